```python
import math
import jax, jax.numpy as jnp
from jax import lax
import numpy as np

D_MODEL = 2048
BATCH = 4
SEQ = 4096
DEPTH = 2

HEAD_DIM = 64
DIFF_HEADS = 8
DIFF_QK = 2 * DIFF_HEADS * HEAD_DIM
DIFF_V_DIM = 2 * HEAD_DIM
DIFF_V = DIFF_HEADS * DIFF_V_DIM
Q_BLOCK = 128
SWA_Q_HEADS = 16
SWA_KV_HEADS = 2
SWA_GROUP = SWA_Q_HEADS // SWA_KV_HEADS
SWA_WINDOW = 128
SWA_BLOCK = 128
SWA_Q = SWA_Q_HEADS * HEAD_DIM
SWA_KV = SWA_KV_HEADS * HEAD_DIM
EVEN_SPLITS = [DIFF_QK, DIFF_QK, DIFF_V, SWA_Q, SWA_KV, SWA_KV]
EVEN_IN = sum(EVEN_SPLITS)
MIX_WIDTH = DIFF_V + SWA_Q
LRU_WIDTH = D_MODEL
LRU_BLOCKS = 8
LRU_BLOCK_DIM = LRU_WIDTH // LRU_BLOCKS
CONV_WIDTH = 4
LRU_C = 8.0
D_FF = 4 * D_MODEL
N_EVEN = (DEPTH + 1) // 2
N_ODD = DEPTH // 2
EPS = 1e-6

kernel_name = "hybrid_diffattn_swa_rglru_block"


def rmsnorm(x, g):
    xf = x.astype(jnp.float32)
    y = xf * lax.rsqrt(jnp.mean(xf * xf, axis=-1, keepdims=True) + EPS)
    return (y * g.astype(jnp.float32)).astype(x.dtype)


def diff_attention(q, k, v, lq1, lk1, lq2, lk2, sub_g, lambda_init):
    B, S = q.shape[0], q.shape[1]
    nb = S // Q_BLOCK
    scale = HEAD_DIM ** -0.5
    f32 = jnp.float32
    lam = (jnp.exp(jnp.sum(lq1.astype(f32) * lk1.astype(f32)))
           - jnp.exp(jnp.sum(lq2.astype(f32) * lk2.astype(f32))) + lambda_init)
    qb = q.reshape(B, nb, Q_BLOCK, 2 * DIFF_HEADS, HEAD_DIM).transpose(1, 0, 2, 3, 4)
    kpos = jnp.arange(S)

    def block(args):
        qi, i = args
        s = jnp.einsum('bqhd,bkhd->bhqk', qi, k).astype(f32) * scale
        qpos = i * Q_BLOCK + jnp.arange(Q_BLOCK)
        causal = kpos[None, :] <= qpos[:, None]
        s = jnp.where(causal[None, None], s, -jnp.inf)
        p = jax.nn.softmax(s, axis=-1).reshape(B, DIFF_HEADS, 2, Q_BLOCK, S)
        w = p[:, :, 0] - lam * p[:, :, 1]
        return jnp.einsum('bhqk,bkhe->bqhe', w.astype(v.dtype), v)

    o = lax.map(block, (qb, jnp.arange(nb)))
    o = o.transpose(1, 0, 2, 3, 4).reshape(B, S, DIFF_HEADS, DIFF_V_DIM)
    o = rmsnorm(o, sub_g) * (1.0 - lambda_init)
    return o.reshape(B, S, DIFF_V)


def swa_with_sinks(q, k, v, sinks):
    B, S = q.shape[0], q.shape[1]
    nb = S // SWA_BLOCK
    f32 = jnp.float32
    qb = q.reshape(B, nb, SWA_BLOCK, SWA_KV_HEADS, SWA_GROUP, HEAD_DIM)

    def band(t):
        tb = t.reshape(B, nb, SWA_BLOCK, SWA_KV_HEADS, HEAD_DIM)
        prev = jnp.pad(tb, ((0, 0), (1, 0), (0, 0), (0, 0), (0, 0)))[:, :-1]
        return jnp.concatenate([prev, tb], axis=2)

    kb, vb = band(k), band(v)
    s = jnp.einsum('bnqhgd,bnkhd->bnhgqk', qb, kb).astype(f32) * (HEAD_DIM ** -0.5)
    qpos = jnp.arange(SWA_BLOCK) + SWA_BLOCK
    kpos = jnp.arange(2 * SWA_BLOCK)
    rel = qpos[:, None] - kpos[None, :]
    in_window = (rel >= 0) & (rel < SWA_WINDOW)
    not_pad = (jnp.arange(nb)[:, None] > 0) | (kpos[None, :] >= SWA_BLOCK)
    mask = in_window[None] & not_pad[:, None, :]
    s = jnp.where(mask[None, :, None, None], s, -jnp.inf)
    sink = sinks.astype(f32).reshape(SWA_KV_HEADS, SWA_GROUP)[None, None, :, :, None, None]
    m = jnp.maximum(jnp.max(s, axis=-1, keepdims=True), sink)
    e = jnp.exp(s - m)
    p = e / (jnp.sum(e, axis=-1, keepdims=True) + jnp.exp(sink - m))
    o = jnp.einsum('bnhgqk,bnkhd->bnqhgd', p.astype(v.dtype), vb)
    return o.reshape(B, S, SWA_Q)


def even_mixer(h, w_in, lq1, lk1, lq2, lk2, sub_g, sinks, w_out, lambda_init):
    B, S, _ = h.shape
    proj = h @ w_in
    idx = [int(v) for v in np.cumsum(EVEN_SPLITS)[:-1]]
    qa, ka, va, qs, ks, vs = jnp.split(proj, idx, axis=-1)
    a_out = diff_attention(
        qa.reshape(B, S, 2 * DIFF_HEADS, HEAD_DIM),
        ka.reshape(B, S, 2 * DIFF_HEADS, HEAD_DIM),
        va.reshape(B, S, DIFF_HEADS, DIFF_V_DIM),
        lq1, lk1, lq2, lk2, sub_g, lambda_init)
    b_out = swa_with_sinks(
        qs.reshape(B, S, SWA_Q_HEADS, HEAD_DIM),
        ks.reshape(B, S, SWA_KV_HEADS, HEAD_DIM),
        vs.reshape(B, S, SWA_KV_HEADS, HEAD_DIM),
        sinks)
    return jnp.concatenate([a_out, b_out], axis=-1) @ w_out


def causal_depthwise_conv(x, w, b):
    y = lax.conv_general_dilated(
        x, w[:, None, :], window_strides=(1,), padding=[(CONV_WIDTH - 1, 0)],
        dimension_numbers=('NWC', 'WIO', 'NWC'), feature_group_count=x.shape[-1])
    return y + b


def rglru_mixer(h, w_in, conv_w, conv_b, gate_w, gate_b, lru_lambda, w_out):
    B, S, _ = h.shape
    f32 = jnp.float32
    proj = h @ w_in
    y_branch, x_branch = jnp.split(proj, 2, axis=-1)
    y_branch = jax.nn.gelu(y_branch)
    xc = causal_depthwise_conv(x_branch, conv_w, conv_b)
    xb = xc.reshape(B, S, LRU_BLOCKS, LRU_BLOCK_DIM)
    gates = jnp.einsum('bsnd,gnde->gbsne', xb, gate_w) + gate_b[:, None, None]
    r = jax.nn.sigmoid(gates[0].astype(f32)).reshape(B, S, LRU_WIDTH)
    i = jax.nn.sigmoid(gates[1].astype(f32)).reshape(B, S, LRU_WIDTH)
    log_a = -LRU_C * r * jax.nn.softplus(-lru_lambda.astype(f32))
    a = jnp.exp(log_a)
    u = jnp.sqrt(-jnp.expm1(2.0 * log_a)) * (i * xc.astype(f32))

    def combine(left, right):
        a1, b1 = left
        a2, b2 = right
        return a1 * a2, a2 * b1 + b2

    _, hs = lax.associative_scan(combine, (a, u), axis=1)
    return (hs.astype(h.dtype) * y_branch) @ w_out


def sq_relu_mlp(h, w1, w2):
    z = jax.nn.relu(h @ w1)
    return (z * z) @ w2


def setup_inputs(seed: int = 0) -> dict:
    key = jax.random.key(seed)
    ks = jax.random.split(key, 24)
    f32 = jnp.float32
    nrm = lambda k, shape, s: jax.random.normal(k, shape, f32) * s
    gain = lambda k, shape: 1.0 + 0.05 * jax.random.normal(k, shape, f32)
    a0 = jax.random.uniform(ks[14], (N_ODD, LRU_WIDTH), f32, 0.9, 0.999)
    sig = a0 ** (1.0 / LRU_C)
    return {
        "x": jax.random.normal(ks[0], (BATCH, SEQ, D_MODEL), f32),
        "even_w_in": nrm(ks[1], (N_EVEN, D_MODEL, EVEN_IN), D_MODEL ** -0.5),
        "even_lam_q1": nrm(ks[2], (N_EVEN, HEAD_DIM), 0.1),
        "even_lam_k1": nrm(ks[3], (N_EVEN, HEAD_DIM), 0.1),
        "even_lam_q2": nrm(ks[4], (N_EVEN, HEAD_DIM), 0.1),
        "even_lam_k2": nrm(ks[5], (N_EVEN, HEAD_DIM), 0.1),
        "even_subln_g": gain(ks[6], (N_EVEN, DIFF_V_DIM)),
        "even_sinks": nrm(ks[7], (N_EVEN, SWA_Q_HEADS), 1.0),
        "even_w_out": nrm(ks[8], (N_EVEN, MIX_WIDTH, D_MODEL), MIX_WIDTH ** -0.5),
        "odd_w_in": nrm(ks[9], (N_ODD, D_MODEL, 2 * LRU_WIDTH), D_MODEL ** -0.5),
        "odd_conv_w": nrm(ks[10], (N_ODD, CONV_WIDTH, LRU_WIDTH), CONV_WIDTH ** -0.5),
        "odd_conv_b": nrm(ks[11], (N_ODD, LRU_WIDTH), 0.02),
        "odd_gate_w": nrm(ks[12], (N_ODD, 2, LRU_BLOCKS, LRU_BLOCK_DIM, LRU_BLOCK_DIM), LRU_BLOCK_DIM ** -0.5),
        "odd_gate_b": nrm(ks[13], (N_ODD, 2, LRU_BLOCKS, LRU_BLOCK_DIM), 0.02),
        "odd_lru_lambda": jnp.log(sig) - jnp.log1p(-sig),
        "odd_w_out": nrm(ks[15], (N_ODD, LRU_WIDTH, D_MODEL), LRU_WIDTH ** -0.5),
        "pre_mix_g": gain(ks[16], (DEPTH, D_MODEL)),
        "post_mix_g": gain(ks[17], (DEPTH, D_MODEL)),
        "pre_mlp_g": gain(ks[18], (DEPTH, D_MODEL)),
        "post_mlp_g": gain(ks[19], (DEPTH, D_MODEL)),
        "mlp_w1": nrm(ks[20], (DEPTH, D_MODEL, D_FF), D_MODEL ** -0.5),
        "mlp_w2": nrm(ks[21], (DEPTH, D_FF, D_MODEL), D_FF ** -0.5),
    }


def reference(x, even_w_in, even_lam_q1, even_lam_k1, even_lam_q2, even_lam_k2,
              even_subln_g, even_sinks, even_w_out, odd_w_in, odd_conv_w, odd_conv_b,
              odd_gate_w, odd_gate_b, odd_lru_lambda, odd_w_out, pre_mix_g, post_mix_g,
              pre_mlp_g, post_mlp_g, mlp_w1, mlp_w2):
    for layer in range(DEPTH):
        h = rmsnorm(x, pre_mix_g[layer])
        if layer % 2 == 0:
            e = layer // 2
            lambda_init = 0.8 - 0.6 * math.exp(-0.3 * layer)
            mix = even_mixer(h, even_w_in[e], even_lam_q1[e], even_lam_k1[e],
                             even_lam_q2[e], even_lam_k2[e], even_subln_g[e],
                             even_sinks[e], even_w_out[e], lambda_init)
        else:
            o = layer // 2
            mix = rglru_mixer(h, odd_w_in[o], odd_conv_w[o], odd_conv_b[o],
                              odd_gate_w[o], odd_gate_b[o], odd_lru_lambda[o], odd_w_out[o])
        x = x + rmsnorm(mix, post_mix_g[layer])
        f = sq_relu_mlp(rmsnorm(x, pre_mlp_g[layer]), mlp_w1[layer], mlp_w2[layer])
        x = x + rmsnorm(f, post_mlp_g[layer])
    return x
```

```python
import functools
import math

import jax
import jax.numpy as jnp
from jax import lax
from jax.experimental import pallas as pl
from jax.experimental.pallas import tpu as pltpu

F32 = jnp.float32
BF16 = jnp.bfloat16

EPS = 1e-6
HEAD_DIM = 64
DIFF_HEADS = 8
SWA_Q_HEADS = 16
SWA_KV_HEADS = 2
SWA_GROUP = SWA_Q_HEADS // SWA_KV_HEADS
SWA_BLOCK = 128
LRU_BLOCKS = 8
CONV_WIDTH = 4
LRU_C = 8.0

LANES = 128
SUBLANES = 8
MIB = 1024 * 1024

NT_DIMS = (((1,), (1,)), ((), ()))


def _params(semantics, vmem_mib):
    return pltpu.CompilerParams(dimension_semantics=semantics,
                                vmem_limit_bytes=vmem_mib * MIB)


def _rmsnorm(x, g):
    ms = jnp.mean(x * x, axis=-1, keepdims=True)
    return x * lax.rsqrt(ms + EPS) * g


def _norm_proj_kernel(x_ref, g_ref, *refs, epilogues):
    n = len(epilogues)
    w_refs, o_refs, hn_ref = refs[:n], refs[n:2 * n], refs[2 * n]

    @pl.when(pl.program_id(1) == 0)
    def _():
        hn_ref[...] = _rmsnorm(x_ref[...], g_ref[...]).astype(BF16)

    hn = hn_ref[...]
    for w_ref, o_ref, epilogue in zip(w_refs, o_refs, epilogues):
        acc = jnp.dot(hn, w_ref[...], preferred_element_type=F32)
        o_ref[...] = epilogue(acc).astype(o_ref.dtype)


def _norm_proj(x, g, ws, epilogues, out_dtypes, *, tm, tn, name):
    m, d = x.shape
    n = ws[0].shape[1]
    nw = len(ws)
    return pl.pallas_call(
        functools.partial(_norm_proj_kernel, epilogues=tuple(epilogues)),
        grid=(m // tm, n // tn),
        in_specs=[pl.BlockSpec((tm, d), lambda i, j: (i, 0)),
                  pl.BlockSpec((1, d), lambda i, j: (0, 0))]
                 + [pl.BlockSpec((d, tn), lambda i, j: (0, j))] * nw,
        out_specs=[pl.BlockSpec((tm, tn), lambda i, j: (i, j))] * nw,
        out_shape=[jax.ShapeDtypeStruct((m, n), dt) for dt in out_dtypes],
        scratch_shapes=[pltpu.VMEM((tm, d), BF16)],
        compiler_params=_params(("parallel", "arbitrary"), 48),
        name=name,
    )(x, g.reshape(1, d), *ws)


def _norm_proj_t_kernel(x_ref, g_ref, wt_ref, o_ref, *, tk):
    hn = _rmsnorm(x_ref[...], g_ref[...]).astype(BF16)
    res = lax.dot_general(wt_ref[...], hn, NT_DIMS, preferred_element_type=F32)
    for c in range(o_ref.shape[0]):
        o_ref[c] = res[:, c * tk:(c + 1) * tk].astype(o_ref.dtype)


def _norm_proj_t(x, g, wt, *, tm, tk, name):
    m, d = x.shape
    n = wt.shape[0]
    return pl.pallas_call(
        functools.partial(_norm_proj_t_kernel, tk=tk),
        grid=(m // tm,),
        in_specs=[pl.BlockSpec((tm, d), lambda i: (i, 0)),
                  pl.BlockSpec((1, d), lambda i: (0, 0)),
                  pl.BlockSpec((n, d), lambda i: (0, 0))],
        out_specs=pl.BlockSpec((tm // tk, n, tk), lambda i: (i, 0, 0)),
        out_shape=jax.ShapeDtypeStruct((m // tk, n, tk), BF16),
        compiler_params=_params(("parallel",), 48),
        name=name,
    )(x, g.reshape(1, d), wt)


def _diff_attn_kernel(lq1_ref, lk1_ref, lq2_ref, lk2_ref, g_ref, q_ref, k_ref, vt_ref,
                      o_ref, qz_ref, m_ref, l_ref, acc_ref, *, tq, lambda_init):
    s_len = q_ref.shape[0]
    tk = tq
    scale = HEAD_DIM ** -0.5
    lam = (jnp.exp(jnp.sum(lq1_ref[...] * lk1_ref[...], keepdims=True))
           - jnp.exp(jnp.sum(lq2_ref[...] * lk2_ref[...], keepdims=True)) + lambda_init)
    first_map = lax.broadcasted_iota(jnp.int32, (tq, LANES), 1) < HEAD_DIM
    kpos = lax.broadcasted_iota(jnp.int32, (tk, 2 * tq), 0)
    qcol = lax.broadcasted_iota(jnp.int32, (tk, 2 * tq), 1)
    causal = kpos <= jnp.where(qcol >= tq, qcol - tq, qcol)

    def tile(j, masked):
        k = k_ref[pl.ds(pl.multiple_of(j * tk, tk), tk), :]
        st = lax.dot_general(k, qz_ref[...], NT_DIMS, preferred_element_type=F32)
        if masked:
            st = jnp.where(causal, st, -jnp.inf)
        m_old = m_ref[...]
        m_new = jnp.maximum(m_old, jnp.max(st, axis=0, keepdims=True))
        alpha = jnp.exp(m_old - m_new)
        p = jnp.exp(st - m_new)
        l_ref[...] = alpha * l_ref[...] + jnp.sum(p, axis=0, keepdims=True)
        acc_ref[...] = alpha * acc_ref[...] + jnp.dot(
            vt_ref[j], p.astype(BF16), preferred_element_type=F32)
        m_ref[...] = m_new

    def q_body(i, carry):
        r0 = pl.multiple_of(i * tq, tq)
        q = q_ref[pl.ds(r0, tq), :].astype(F32) * scale
        qz_ref[0:tq, :] = jnp.where(first_map, q, 0.0).astype(BF16)
        qz_ref[tq:2 * tq, :] = jnp.where(first_map, 0.0, q).astype(BF16)
        m_ref[...] = jnp.full(m_ref.shape, -jnp.inf, F32)
        l_ref[...] = jnp.zeros(l_ref.shape, F32)
        acc_ref[...] = jnp.zeros(acc_ref.shape, F32)

        def k_body(j, c):
            tile(j, False)
            return c

        lax.fori_loop(0, i, k_body, 0)
        tile(i, True)

        inv_l = 1.0 / l_ref[...]
        acc = acc_ref[...]
        ot = acc[:, :tq] * inv_l[:, :tq] - lam * (acc[:, tq:] * inv_l[:, tq:])
        o = _rmsnorm(ot.T, g_ref[...]) * (1.0 - lambda_init)
        o_ref[pl.ds(r0, tq), :] = o.astype(o_ref.dtype)
        return carry

    lax.fori_loop(0, s_len // tq, q_body, 0)


def _diff_attn(proj, vt, lq1, lk1, lq2, lk2, sub_g, *, batch, seq, tq, lambda_init, name):
    vec = pl.BlockSpec((1, HEAD_DIM), lambda b, h: (0, 0))
    return pl.pallas_call(
        functools.partial(_diff_attn_kernel, tq=tq, lambda_init=lambda_init),
        grid=(batch, DIFF_HEADS),
        in_specs=[vec, vec, vec, vec,
                  pl.BlockSpec((1, LANES), lambda b, h: (0, 0)),
                  pl.BlockSpec((None, seq, LANES), lambda b, h: (b, 0, h)),
                  pl.BlockSpec((None, seq, LANES), lambda b, h: (b, 0, DIFF_HEADS + h)),
                  pl.BlockSpec((seq // tq, LANES, tq), lambda b, h: (b, h, 0))],
        out_specs=pl.BlockSpec((None, seq, LANES), lambda b, h: (b, 0, h)),
        out_shape=jax.ShapeDtypeStruct((batch, seq, DIFF_HEADS * LANES), BF16),
        scratch_shapes=[pltpu.VMEM((2 * tq, LANES), BF16),
                        pltpu.VMEM((1, 2 * tq), F32),
                        pltpu.VMEM((1, 2 * tq), F32),
                        pltpu.VMEM((LANES, 2 * tq), F32)],
        compiler_params=_params(("parallel", "parallel"), 32),
        name=name,
    )(lq1.reshape(1, -1), lk1.reshape(1, -1), lq2.reshape(1, -1), lk2.reshape(1, -1),
      sub_g.reshape(1, -1), proj, proj, vt)


def _swa_kernel(sinks_ref, q_ref, kk_ref, vv_ref, o_ref):
    s_len = q_ref.shape[0]
    blk = SWA_BLOCK
    kvh = pl.program_id(1)
    scale = HEAD_DIM ** -0.5
    first_head = lax.broadcasted_iota(jnp.int32, (blk, LANES), 1) < HEAD_DIM
    sinks = [sinks_ref[kvh * SWA_GROUP + g] for g in range(SWA_GROUP)]

    def block(r0, k0, nkeys):
        qparts = []
        for gp in range(SWA_GROUP // 2):
            qq = q_ref[pl.ds(r0, blk), gp * LANES:(gp + 1) * LANES].astype(F32) * scale
            qparts.append(jnp.where(first_head, qq, 0.0).astype(BF16))
            qparts.append(jnp.where(first_head, 0.0, qq).astype(BF16))
        qs = jnp.concatenate(qparts, axis=0)
        kk = kk_ref[pl.ds(k0, nkeys), :]
        vv = vv_ref[pl.ds(k0, nkeys), :]
        s = lax.dot_general(qs, kk, NT_DIMS, preferred_element_type=F32)
        row = lax.broadcasted_iota(jnp.int32, (blk, nkeys), 0)
        col = lax.broadcasted_iota(jnp.int32, (blk, nkeys), 1)
        if nkeys == 2 * blk:
            valid = jnp.logical_and(col > row, col <= row + blk)
        else:
            valid = col <= row
        ps = []
        for g in range(SWA_GROUP):
            sg = jnp.where(valid, s[g * blk:(g + 1) * blk], -jnp.inf)
            m = jnp.maximum(jnp.max(sg, axis=-1, keepdims=True), sinks[g])
            e = jnp.exp(sg - m)
            denom = jnp.sum(e, axis=-1, keepdims=True) + jnp.exp(sinks[g] - m)
            ps.append((e * (1.0 / denom)).astype(BF16))
        p = jnp.concatenate(ps, axis=0)
        o = jnp.dot(p, vv, preferred_element_type=F32)
        for gp in range(SWA_GROUP // 2):
            lo = o[(2 * gp) * blk:(2 * gp + 1) * blk]
            hi = o[(2 * gp + 1) * blk:(2 * gp + 2) * blk]
            o_ref[pl.ds(r0, blk), gp * LANES:(gp + 1) * LANES] = jnp.where(
                first_head, lo, hi).astype(o_ref.dtype)

    block(0, 0, blk)

    def body(n, c):
        r0 = pl.multiple_of(n * blk, blk)
        block(r0, pl.multiple_of(r0 - blk, blk), 2 * blk)
        return c

    lax.fori_loop(1, s_len // blk, body, 0)


def _swa(proj, sinks, *, batch, seq, q_col, k_col, v_col, name):
    qw = SWA_GROUP * HEAD_DIM
    return pl.pallas_call(
        _swa_kernel,
        grid=(batch, SWA_KV_HEADS),
        in_specs=[pl.BlockSpec(memory_space=pltpu.SMEM),
                  pl.BlockSpec((None, seq, qw), lambda b, h: (b, 0, q_col + h)),
                  pl.BlockSpec((None, seq, LANES), lambda b, h: (b, 0, k_col + h)),
                  pl.BlockSpec((None, seq, LANES), lambda b, h: (b, 0, v_col + h))],
        out_specs=pl.BlockSpec((None, seq, qw), lambda b, h: (b, 0, h)),
        out_shape=jax.ShapeDtypeStruct((batch, seq, SWA_Q_HEADS * HEAD_DIM), BF16),
        compiler_params=_params(("parallel", "parallel"), 40),
        name=name,
    )(sinks, proj, proj, proj)


def _out_proj_kernel(*refs, n_in):
    a_refs, w_refs = refs[:n_in], refs[n_in:2 * n_in]
    x_ref, g_ref, o_ref = refs[2 * n_in:]
    mix = jnp.dot(a_refs[0][...], w_refs[0][...], preferred_element_type=F32)
    for a_ref, w_ref in zip(a_refs[1:], w_refs[1:]):
        mix = mix + jnp.dot(a_ref[...], w_ref[...], preferred_element_type=F32)
    o_ref[...] = x_ref[...] + _rmsnorm(mix, g_ref[...])


def _out_proj(acts, ws, x, g, *, tm, name):
    m, d = x.shape
    n_in = len(acts)
    return pl.pallas_call(
        functools.partial(_out_proj_kernel, n_in=n_in),
        grid=(m // tm,),
        in_specs=[pl.BlockSpec((tm, a.shape[1]), lambda i: (i, 0)) for a in acts]
                 + [pl.BlockSpec(w.shape, lambda i: (0, 0)) for w in ws]
                 + [pl.BlockSpec((tm, d), lambda i: (i, 0)),
                    pl.BlockSpec((1, d), lambda i: (0, 0))],
        out_specs=pl.BlockSpec((tm, d), lambda i: (i, 0)),
        out_shape=jax.ShapeDtypeStruct((m, d), F32),
        compiler_params=_params(("parallel",), 52),
        name=name,
    )(*acts, *ws, x, g.reshape(1, d))


def _mlp_kernel(x_ref, g1_ref, w1_ref, w2_ref, g2_ref, o_ref, hn_ref, acc_ref):
    j = pl.program_id(1)

    @pl.when(j == 0)
    def _():
        hn_ref[...] = _rmsnorm(x_ref[...], g1_ref[...]).astype(BF16)

    z = jnp.maximum(jnp.dot(hn_ref[...], w1_ref[...], preferred_element_type=F32), 0.0)
    contrib = jnp.dot((z * z).astype(BF16), w2_ref[...], preferred_element_type=F32)

    @pl.when(j == 0)
    def _():
        acc_ref[...] = contrib

    @pl.when(j > 0)
    def _():
        acc_ref[...] += contrib

    @pl.when(j == pl.num_programs(1) - 1)
    def _():
        o_ref[...] = x_ref[...] + _rmsnorm(acc_ref[...], g2_ref[...])


def _mlp(x, g1, w1, w2, g2, *, tm, tf, name):
    m, d = x.shape
    f = w1.shape[1]
    return pl.pallas_call(
        _mlp_kernel,
        grid=(m // tm, f // tf),
        in_specs=[pl.BlockSpec((tm, d), lambda i, j: (i, 0)),
                  pl.BlockSpec((1, d), lambda i, j: (0, 0)),
                  pl.BlockSpec((d, tf), lambda i, j: (0, j)),
                  pl.BlockSpec((tf, d), lambda i, j: (j, 0)),
                  pl.BlockSpec((1, d), lambda i, j: (0, 0))],
        out_specs=pl.BlockSpec((tm, d), lambda i, j: (i, 0)),
        out_shape=jax.ShapeDtypeStruct((m, d), F32),
        scratch_shapes=[pltpu.VMEM((tm, d), BF16), pltpu.VMEM((tm, d), F32)],
        compiler_params=_params(("parallel", "arbitrary"), 52),
        name=name,
    )(x, g1.reshape(1, d), w1, w2, g2.reshape(1, d))


def _rglru_kernel(x_ref, y_ref, cw_ref, cb_ref, wr_ref, wi_ref, br_ref, bi_ref, lam_ref,
                  o_ref, xp_ref, a_ref, u_ref, *, chunk):
    s_len, c = x_ref.shape
    pad = SUBLANES
    xp_ref[0:pad, :] = jnp.zeros((pad, c), F32)
    xp_ref[pad:, :] = x_ref[...]
    softplus_neg = jax.nn.softplus(-lam_ref[...])
    cw = cw_ref[...]
    row_in_group = lax.broadcasted_iota(jnp.int32, (chunk, c), 0) % SUBLANES

    def chunk_body(ci, h):
        t0 = pl.multiple_of(ci * chunk, chunk)
        xh = xp_ref[pl.ds(t0, chunk + pad), :]
        xc = cb_ref[...] + cw[CONV_WIDTH - 1:CONV_WIDTH] * xh[pad:]
        for d in range(1, CONV_WIDTH):
            tap = cw[CONV_WIDTH - 1 - d:CONV_WIDTH - d]
            xc = xc + tap * pltpu.roll(xh, d, 0)[pad:]
        xcb = xc.astype(BF16)
        r = jax.nn.sigmoid(jnp.dot(xcb, wr_ref[...], preferred_element_type=F32) + br_ref[...])
        ig = jax.nn.sigmoid(jnp.dot(xcb, wi_ref[...], preferred_element_type=F32) + bi_ref[...])
        a = jnp.exp(-LRU_C * r * softplus_neg)
        u = jnp.sqrt(1.0 - a * a) * (ig * xc)
        for d in (1, 2, 4):
            keep = row_in_group >= d
            a_prev = jnp.where(keep, pltpu.roll(a, d, 0), 1.0)
            u_prev = jnp.where(keep, pltpu.roll(u, d, 0), 0.0)
            u = a * u_prev + u
            a = a * a_prev
        a_ref[...] = a
        u_ref[...] = u

        def group_body(gi, h):
            g0 = pl.multiple_of(gi * SUBLANES, SUBLANES)
            hv = a_ref[pl.ds(g0, SUBLANES), :] * h + u_ref[pl.ds(g0, SUBLANES), :]
            u_ref[pl.ds(g0, SUBLANES), :] = hv
            return jnp.broadcast_to(hv[SUBLANES - 1:SUBLANES, :], (SUBLANES, c))

        h = lax.fori_loop(0, chunk // SUBLANES, group_body, h, unroll=8)
        o_ref[pl.ds(t0, chunk), :] = (
            u_ref[...] * y_ref[pl.ds(t0, chunk), :].astype(F32)).astype(o_ref.dtype)
        return h

    lax.fori_loop(0, s_len // chunk, chunk_body, jnp.zeros((SUBLANES, c), F32))


def _rglru(xb, yb, conv_w, conv_b, gate_w, gate_b, lru_lambda, *, batch, seq, chunk, name):
    width = xb.shape[-1]
    c = width // LRU_BLOCKS
    row = pl.BlockSpec((1, c), lambda b, n: (0, n))
    gate = lambda g: pl.BlockSpec((None, None, c, c), lambda b, n: (g, n, 0, 0))
    gate_bias = lambda g: pl.BlockSpec((None, 1, c), lambda b, n: (g, 0, n))
    return pl.pallas_call(
        functools.partial(_rglru_kernel, chunk=chunk),
        grid=(batch, LRU_BLOCKS),
        in_specs=[pl.BlockSpec((None, seq, c), lambda b, n: (b, 0, n)),
                  pl.BlockSpec((None, seq, c), lambda b, n: (b, 0, n)),
                  pl.BlockSpec((CONV_WIDTH, c), lambda b, n: (0, n)),
                  row, gate(0), gate(1), gate_bias(0), gate_bias(1), row],
        out_specs=pl.BlockSpec((None, seq, c), lambda b, n: (b, 0, n)),
        out_shape=jax.ShapeDtypeStruct((batch, seq, width), BF16),
        scratch_shapes=[pltpu.VMEM((seq + SUBLANES, c), F32),
                        pltpu.VMEM((chunk, c), F32),
                        pltpu.VMEM((chunk, c), F32)],
        compiler_params=_params(("parallel", "parallel"), 40),
        name=name,
    )(xb.reshape(batch, seq, width), yb.reshape(batch, seq, width), conv_w,
      conv_b.reshape(1, width), gate_w, gate_w, gate_b.reshape(2, 1, width),
      gate_b.reshape(2, 1, width), lru_lambda.reshape(1, width))


def _identity(v):
    return v


def _even_mixer(xf, g_pre, w_in, lq1, lk1, lq2, lk2, sub_g, sinks, w_out, g_post,
                *, batch, seq, lambda_init):
    m = batch * seq
    n_qk = 2 * DIFF_HEADS * HEAD_DIM
    n_v = DIFF_HEADS * 2 * HEAD_DIM
    n_sq = SWA_Q_HEADS * HEAD_DIM
    c0 = 2 * n_qk + n_v + n_sq
    ks = [w_in[:, c0 + h * HEAD_DIM:c0 + (h + 1) * HEAD_DIM] for h in range(SWA_KV_HEADS)]
    c1 = c0 + SWA_KV_HEADS * HEAD_DIM
    vs = [w_in[:, c1 + h * HEAD_DIM:c1 + (h + 1) * HEAD_DIM] for h in range(SWA_KV_HEADS)]
    w_main = jnp.concatenate(
        [w_in[:, :2 * n_qk], w_in[:, 2 * n_qk + n_v:c0]]
        + [t for k in ks for t in (k, k)] + [t for v in vs for t in (v, v)],
        axis=1).astype(BF16)
    w_vt = w_in[:, 2 * n_qk:2 * n_qk + n_v].T.astype(BF16)

    tq = min(256, seq)
    (proj,) = _norm_proj(xf, g_pre, [w_main], [_identity], [BF16],
                         tm=min(512, m), tn=512, name="even_in_proj")
    vt = _norm_proj_t(xf, g_pre, w_vt, tm=min(512, m), tk=tq, name="even_v_proj_t")
    proj3 = proj.reshape(batch, seq, proj.shape[1])
    a_out = _diff_attn(proj3, vt, lq1, lk1, lq2, lk2, sub_g, batch=batch, seq=seq, tq=tq,
                       lambda_init=lambda_init, name="diff_attn")
    swa_q_col = 2 * n_qk // (SWA_GROUP * HEAD_DIM)
    swa_k_col = (2 * n_qk + n_sq) // LANES
    b_out = _swa(proj3, sinks, batch=batch, seq=seq, q_col=swa_q_col, k_col=swa_k_col,
                 v_col=swa_k_col + SWA_KV_HEADS, name="swa")
    w_o = w_out.astype(BF16)
    return _out_proj([a_out.reshape(m, n_v), b_out.reshape(m, n_sq)],
                     [w_o[:n_v], w_o[n_v:]], xf, g_post, tm=min(512, m), name="even_out_proj")


def _odd_mixer(xf, g_pre, w_in, conv_w, conv_b, gate_w, gate_b, lru_lambda, w_out, g_post,
               *, batch, seq):
    m = batch * seq
    width = w_in.shape[1] // 2
    w_b = w_in.astype(BF16)
    yb, xb = _norm_proj(xf, g_pre, [w_b[:, :width], w_b[:, width:]],
                        [jax.nn.gelu, _identity], [BF16, F32],
                        tm=min(512, m), tn=512, name="odd_in_proj")
    hy = _rglru(xb, yb, conv_w, conv_b, gate_w.astype(BF16), gate_b, lru_lambda,
                batch=batch, seq=seq, chunk=min(512, seq), name="rglru")
    return _out_proj([hy.reshape(m, width)], [w_out.astype(BF16)], xf, g_post,
                     tm=min(512, m), name="odd_out_proj")


def kernel(x, even_w_in, even_lam_q1, even_lam_k1, even_lam_q2, even_lam_k2, even_subln_g,
           even_sinks, even_w_out, odd_w_in, odd_conv_w, odd_conv_b, odd_gate_w, odd_gate_b,
           odd_lru_lambda, odd_w_out, pre_mix_g, post_mix_g, pre_mlp_g, post_mlp_g, mlp_w1,
           mlp_w2):
    batch, seq, d = x.shape
    m = batch * seq
    xf = x.reshape(m, d)
    depth = pre_mix_g.shape[0]
    for layer in range(depth):
        if layer % 2 == 0:
            e = layer // 2
            lambda_init = 0.8 - 0.6 * math.exp(-0.3 * layer)
            xf = _even_mixer(xf, pre_mix_g[layer], even_w_in[e], even_lam_q1[e],
                             even_lam_k1[e], even_lam_q2[e], even_lam_k2[e], even_subln_g[e],
                             even_sinks[e], even_w_out[e], post_mix_g[layer],
                             batch=batch, seq=seq, lambda_init=lambda_init)
        else:
            o = layer // 2
            xf = _odd_mixer(xf, pre_mix_g[layer], odd_w_in[o], odd_conv_w[o], odd_conv_b[o],
                            odd_gate_w[o], odd_gate_b[o], odd_lru_lambda[o], odd_w_out[o],
                            post_mix_g[layer], batch=batch, seq=seq)
        xf = _mlp(xf, pre_mlp_g[layer], mlp_w1[layer].astype(BF16), mlp_w2[layer].astype(BF16),
                  post_mlp_g[layer], tm=min(512, m), tf=512, name=f"mlp_{layer}")
    return xf.reshape(batch, seq, d)
```

```python
import functools
import math

import jax
import jax.numpy as jnp
from jax import lax
from jax.experimental import pallas as pl
from jax.experimental.pallas import tpu as pltpu

F32 = jnp.float32
BF16 = jnp.bfloat16

EPS = 1e-6
HEAD_DIM = 64
DIFF_HEADS = 8
SWA_Q_HEADS = 16
SWA_KV_HEADS = 2
SWA_GROUP = SWA_Q_HEADS // SWA_KV_HEADS
SWA_BLOCK = 128
LRU_BLOCKS = 8
CONV_WIDTH = 4
LRU_C = 8.0

LANES = 128
SUBLANES = 8
MIB = 1024 * 1024

NT_DIMS = (((1,), (1,)), ((), ()))


def _params(semantics, vmem_mib):
    return pltpu.CompilerParams(dimension_semantics=semantics,
                                vmem_limit_bytes=vmem_mib * MIB)


def _rmsnorm(x, g):
    ms = jnp.mean(x * x, axis=-1, keepdims=True)
    return x * lax.rsqrt(ms + EPS) * g


def _resident(shape):
    return pl.BlockSpec(shape, lambda *_: (0,) * len(shape), pipeline_mode=pl.Buffered(1))


def _norm_proj_kernel(x_ref, g_ref, *refs, epilogues, col_scales, transposed, tn, tk):
    n = len(epilogues)
    w_refs, o_refs = refs[:n], refs[n:2 * n]
    x = x_ref[...]
    rs = lax.rsqrt(jnp.mean(x * x, axis=-1, keepdims=True) + EPS)
    xg = (x * g_ref[...]).astype(BF16)
    for w_ref, o_ref, epilogue, scales, tr in zip(w_refs, o_refs, epilogues, col_scales,
                                                  transposed):
        for c0 in range(0, w_ref.shape[1], tn):
            row_scale = rs if scales is None else rs * scales[c0 // tn]
            res = epilogue(
                jnp.dot(xg, w_ref[:, c0:c0 + tn], preferred_element_type=F32) * row_scale)
            if tr:
                res_t = res.T.astype(o_ref.dtype)
                for c in range(o_ref.shape[0]):
                    o_ref[c, c0:c0 + tn, :] = res_t[:, c * tk:(c + 1) * tk]
            else:
                o_ref[:, c0:c0 + tn] = res.astype(o_ref.dtype)


def _norm_proj(x, g, ws, epilogues, out_dtypes, *, tm, tn, name, col_scales=None,
               transposed=None, tk=None):
    m, d = x.shape
    transposed = transposed or (False,) * len(ws)
    col_scales = col_scales or (None,) * len(ws)
    out_specs, out_shape = [], []
    for w, dt, tr in zip(ws, out_dtypes, transposed):
        n = w.shape[1]
        if tr:
            out_specs.append(pl.BlockSpec((tm // tk, n, tk), lambda i: (i, 0, 0)))
            out_shape.append(jax.ShapeDtypeStruct((m // tk, n, tk), dt))
        else:
            out_specs.append(pl.BlockSpec((tm, n), lambda i: (i, 0)))
            out_shape.append(jax.ShapeDtypeStruct((m, n), dt))
    return pl.pallas_call(
        functools.partial(_norm_proj_kernel, epilogues=tuple(epilogues),
                          col_scales=tuple(col_scales), transposed=tuple(transposed),
                          tn=tn, tk=tk),
        grid=(m // tm,),
        in_specs=[pl.BlockSpec((tm, d), lambda i: (i, 0)), _resident((1, d))]
                 + [_resident(w.shape) for w in ws],
        out_specs=out_specs,
        out_shape=out_shape,
        compiler_params=_params(("parallel",), 52),
        name=name,
    )(x, g.reshape(1, d), *ws)


DIFF_HEADS_PER_STEP = 4
LOG2_E = 1.4426950408889634
QK_SCALE_LOG2 = HEAD_DIM ** -0.5 * LOG2_E
SUM_ROWS = 16


def _diff_attn_kernel(lq1_ref, lk1_ref, lq2_ref, lk2_ref, g_ref, q_ref, k_ref, vt_ref,
                      o_ref, *scratch, tq, lambda_init):
    hps = DIFF_HEADS_PER_STEP
    s_refs, qzt_refs, m_refs, acc_refs = (scratch[n * hps:(n + 1) * hps] for n in range(4))
    s_len = q_ref.shape[0]
    tk = tq
    heads = range(hps)
    ones_rows = jnp.ones((SUM_ROWS, tk), BF16)
    lam = (jnp.exp(jnp.sum(lq1_ref[...] * lk1_ref[...], keepdims=True))
           - jnp.exp(jnp.sum(lq2_ref[...] * lk2_ref[...], keepdims=True)) + lambda_init)
    first_map = lax.broadcasted_iota(jnp.int32, (LANES, tq), 0) < HEAD_DIM
    kpos = lax.broadcasted_iota(jnp.int32, (tk, 2 * tq), 0)
    qcol = lax.broadcasted_iota(jnp.int32, (tk, 2 * tq), 1)
    causal = kpos <= jnp.where(qcol >= tq, qcol - tq, qcol)

    def scores(hh, j):
        cols = slice(hh * LANES, (hh + 1) * LANES)
        k = k_ref[pl.ds(pl.multiple_of(j * tk, tk), tk), cols]
        s_refs[hh][...] = jnp.dot(k, qzt_refs[hh][...], preferred_element_type=F32)

    def consume(hh, j, masked):
        cols = slice(hh * LANES, (hh + 1) * LANES)
        st = s_refs[hh][...]
        if masked:
            st = jnp.where(causal, st, -jnp.inf)
        m_old = m_refs[hh][...]
        m_new = jnp.maximum(m_old, jnp.max(st, axis=0, keepdims=True))
        alpha = jnp.exp2(m_old - m_new)
        p = jnp.exp2(st - m_new).astype(BF16)
        vt_ones = jnp.concatenate([vt_ref[j, cols, :], ones_rows], axis=0)
        acc_refs[hh][...] = alpha * acc_refs[hh][...] + jnp.dot(
            vt_ones, p, preferred_element_type=F32)
        m_refs[hh][...] = m_new

    def q_body(i, carry):
        r0 = pl.multiple_of(i * tq, tq)
        for hh in heads:
            qt = q_ref[pl.ds(r0, tq), hh * LANES:(hh + 1) * LANES].astype(F32).T
            qzt_refs[hh][:, 0:tq] = jnp.where(first_map, qt, 0.0).astype(BF16)
            qzt_refs[hh][:, tq:2 * tq] = jnp.where(first_map, 0.0, qt).astype(BF16)
            m_refs[hh][...] = jnp.full(m_refs[hh].shape, -jnp.inf, F32)
            acc_refs[hh][...] = jnp.zeros(acc_refs[hh].shape, F32)
            scores(hh, 0)

        def k_body(j, c):
            for hh in heads:
                consume(hh, j, False)
                scores(hh, j + 1)
            return c

        lax.fori_loop(0, i, k_body, 0)
        for hh in heads:
            consume(hh, i, True)

        for hh in heads:
            acc = acc_refs[hh][0:LANES, :]
            inv_l = 1.0 / acc_refs[hh][LANES:LANES + 1, :]
            ot = acc[:, :tq] * inv_l[:, :tq] - lam * (acc[:, tq:] * inv_l[:, tq:])
            o = _rmsnorm(ot.T, g_ref[...]) * (1.0 - lambda_init)
            o_ref[pl.ds(r0, tq), hh * LANES:(hh + 1) * LANES] = o.astype(o_ref.dtype)
        return carry

    lax.fori_loop(0, s_len // tq, q_body, 0)


def _diff_attn(proj, vt, lq1, lk1, lq2, lk2, sub_g, *, batch, seq, tq, lambda_init, name):
    hps = DIFF_HEADS_PER_STEP
    width = hps * LANES
    groups = DIFF_HEADS // hps
    vec = pl.BlockSpec((1, HEAD_DIM), lambda b, h: (0, 0))
    return pl.pallas_call(
        functools.partial(_diff_attn_kernel, tq=tq, lambda_init=lambda_init),
        grid=(batch, groups),
        in_specs=[vec, vec, vec, vec,
                  pl.BlockSpec((1, LANES), lambda b, h: (0, 0)),
                  pl.BlockSpec((None, seq, width), lambda b, h: (b, 0, h)),
                  pl.BlockSpec((None, seq, width), lambda b, h: (b, 0, groups + h)),
                  pl.BlockSpec((seq // tq, width, tq), lambda b, h: (b, h, 0))],
        out_specs=pl.BlockSpec((None, seq, width), lambda b, h: (b, 0, h)),
        out_shape=jax.ShapeDtypeStruct((batch, seq, DIFF_HEADS * LANES), BF16),
        scratch_shapes=[pltpu.VMEM((tq, 2 * tq), F32)] * hps
                       + [pltpu.VMEM((LANES, 2 * tq), BF16)] * hps
                       + [pltpu.VMEM((1, 2 * tq), F32)] * hps
                       + [pltpu.VMEM((LANES + SUM_ROWS, 2 * tq), F32)] * hps,
        compiler_params=_params(("parallel", "parallel"), 48),
        name=name,
    )(lq1.reshape(1, -1), lk1.reshape(1, -1), lq2.reshape(1, -1), lk2.reshape(1, -1),
      sub_g.reshape(1, -1), proj, proj, vt)


def _swa_kernel(sinks_ref, q_ref, kk_ref, vv_ref, o_ref):
    s_len = q_ref.shape[0]
    blk = SWA_BLOCK
    kvh = pl.program_id(1)
    first_head = lax.broadcasted_iota(jnp.int32, (blk, LANES), 1) < HEAD_DIM
    sinks = [sinks_ref[kvh * SWA_GROUP + g] * LOG2_E for g in range(SWA_GROUP)]

    def block(r0, k0, nkeys):
        qparts = []
        for gp in range(SWA_GROUP // 2):
            qq = q_ref[pl.ds(r0, blk), gp * LANES:(gp + 1) * LANES].astype(F32)
            qparts.append(jnp.where(first_head, qq, 0.0).astype(BF16))
            qparts.append(jnp.where(first_head, 0.0, qq).astype(BF16))
        qs = jnp.concatenate(qparts, axis=0)
        kk = kk_ref[pl.ds(k0, nkeys), :]
        vv = vv_ref[pl.ds(k0, nkeys), :]
        s = lax.dot_general(qs, kk, NT_DIMS, preferred_element_type=F32)
        row = lax.broadcasted_iota(jnp.int32, (blk, nkeys), 0)
        col = lax.broadcasted_iota(jnp.int32, (blk, nkeys), 1)
        if nkeys == 2 * blk:
            valid = jnp.logical_and(col > row, col <= row + blk)
        else:
            valid = col <= row
        ps = []
        for g in range(SWA_GROUP):
            sg = jnp.where(valid, s[g * blk:(g + 1) * blk], -jnp.inf)
            m = jnp.maximum(jnp.max(sg, axis=-1, keepdims=True), sinks[g])
            e = jnp.exp2(sg - m)
            denom = jnp.sum(e, axis=-1, keepdims=True) + jnp.exp2(sinks[g] - m)
            ps.append((e * (1.0 / denom)).astype(BF16))
        p = jnp.concatenate(ps, axis=0)
        o = jnp.dot(p, vv, preferred_element_type=F32)
        for gp in range(SWA_GROUP // 2):
            lo = o[(2 * gp) * blk:(2 * gp + 1) * blk]
            hi = o[(2 * gp + 1) * blk:(2 * gp + 2) * blk]
            o_ref[pl.ds(r0, blk), gp * LANES:(gp + 1) * LANES] = jnp.where(
                first_head, lo, hi).astype(o_ref.dtype)

    block(0, 0, blk)

    def body(n, c):
        r0 = pl.multiple_of(n * blk, blk)
        block(r0, pl.multiple_of(r0 - blk, blk), 2 * blk)
        return c

    lax.fori_loop(1, s_len // blk, body, 0)


def _swa(proj, sinks, *, batch, seq, q_col, k_col, v_col, name):
    qw = SWA_GROUP * HEAD_DIM
    return pl.pallas_call(
        _swa_kernel,
        grid=(batch, SWA_KV_HEADS),
        in_specs=[pl.BlockSpec(memory_space=pltpu.SMEM),
                  pl.BlockSpec((None, seq, qw), lambda b, h: (b, 0, q_col + h)),
                  pl.BlockSpec((None, seq, LANES), lambda b, h: (b, 0, k_col + h)),
                  pl.BlockSpec((None, seq, LANES), lambda b, h: (b, 0, v_col + h))],
        out_specs=pl.BlockSpec((None, seq, qw), lambda b, h: (b, 0, h)),
        out_shape=jax.ShapeDtypeStruct((batch, seq, SWA_Q_HEADS * HEAD_DIM), BF16),
        compiler_params=_params(("parallel", "parallel"), 40),
        name=name,
    )(sinks, proj, proj, proj)


def _out_proj_kernel(*refs, n_in, sub):
    a_refs, w_refs = refs[:n_in], refs[n_in:2 * n_in]
    x_ref, g_ref, o_ref = refs[2 * n_in:]
    for r0 in range(0, x_ref.shape[0], sub):
        rows = slice(r0, r0 + sub)
        mix = jnp.dot(a_refs[0][rows, :], w_refs[0][...], preferred_element_type=F32)
        for a_ref, w_ref in zip(a_refs[1:], w_refs[1:]):
            mix = mix + jnp.dot(a_ref[rows, :], w_ref[...], preferred_element_type=F32)
        o_ref[rows, :] = x_ref[rows, :] + _rmsnorm(mix, g_ref[...])


def _out_proj(acts, ws, x, g, *, tm, sub, name):
    m, d = x.shape
    n_in = len(acts)
    return pl.pallas_call(
        functools.partial(_out_proj_kernel, n_in=n_in, sub=sub),
        grid=(m // tm,),
        in_specs=[pl.BlockSpec((tm, a.shape[1]), lambda i: (i, 0)) for a in acts]
                 + [_resident(w.shape) for w in ws]
                 + [pl.BlockSpec((tm, d), lambda i: (i, 0)), _resident((1, d))],
        out_specs=pl.BlockSpec((tm, d), lambda i: (i, 0)),
        out_shape=jax.ShapeDtypeStruct((m, d), F32),
        compiler_params=_params(("parallel",), 52),
        name=name,
    )(*acts, *ws, x, g.reshape(1, d))


def _mlp_kernel(x_ref, g1_ref, w1_ref, w2_ref, g2_ref, o_ref, hn_ref, acc_ref):
    j = pl.program_id(1)

    @pl.when(j == 0)
    def _():
        hn_ref[...] = _rmsnorm(x_ref[...], g1_ref[...]).astype(BF16)
        acc_ref[...] = jnp.zeros(acc_ref.shape, F32)

    z = jnp.maximum(jnp.dot(hn_ref[...], w1_ref[...], preferred_element_type=F32), 0.0)
    acc_ref[...] += jnp.dot((z * z).astype(BF16), w2_ref[...], preferred_element_type=F32)

    @pl.when(j == pl.num_programs(1) - 1)
    def _():
        o_ref[...] = x_ref[...] + _rmsnorm(acc_ref[...], g2_ref[...])


def _mlp(x, g1, w1, w2, g2, *, tm, tf, name):
    m, d = x.shape
    f = w1.shape[1]
    return pl.pallas_call(
        _mlp_kernel,
        grid=(m // tm, f // tf),
        in_specs=[pl.BlockSpec((tm, d), lambda i, j: (i, 0)),
                  _resident((1, d)),
                  pl.BlockSpec((d, tf), lambda i, j: (0, j)),
                  pl.BlockSpec((tf, d), lambda i, j: (j, 0)),
                  _resident((1, d))],
        out_specs=pl.BlockSpec((tm, d), lambda i, j: (i, 0)),
        out_shape=jax.ShapeDtypeStruct((m, d), F32),
        scratch_shapes=[pltpu.VMEM((tm, d), BF16), pltpu.VMEM((tm, d), F32)],
        compiler_params=_params(("parallel", "arbitrary"), 52),
        name=name,
    )(x, g1.reshape(1, d), w1, w2, g2.reshape(1, d))


def _rglru_kernel(x_ref, y_ref, cw_ref, cb_ref, wr_ref, wi_ref, br_ref, bi_ref, lam_ref,
                  o_ref, xp_ref, a_ref, u_ref, *, chunk):
    s_len, c = x_ref.shape
    pad = SUBLANES
    xp_ref[0:pad, :] = jnp.zeros((pad, c), F32)
    xp_ref[pad:, :] = x_ref[...]
    softplus_neg = jax.nn.softplus(-lam_ref[...])
    cw = cw_ref[...]
    row_in_group = lax.broadcasted_iota(jnp.int32, (chunk, c), 0) % SUBLANES

    def chunk_body(ci, h):
        t0 = pl.multiple_of(ci * chunk, chunk)
        xh = xp_ref[pl.ds(t0, chunk + pad), :]
        xc = cb_ref[...] + cw[CONV_WIDTH - 1:CONV_WIDTH] * xh[pad:]
        for d in range(1, CONV_WIDTH):
            tap = cw[CONV_WIDTH - 1 - d:CONV_WIDTH - d]
            xc = xc + tap * pltpu.roll(xh, d, 0)[pad:]
        xcb = xc.astype(BF16)
        r = jax.nn.sigmoid(jnp.dot(xcb, wr_ref[...], preferred_element_type=F32) + br_ref[...])
        ig = jax.nn.sigmoid(jnp.dot(xcb, wi_ref[...], preferred_element_type=F32) + bi_ref[...])
        a = jnp.exp(-LRU_C * r * softplus_neg)
        u = jnp.sqrt(1.0 - a * a) * (ig * xc)
        for d in (1, 2, 4):
            keep = row_in_group >= d
            a_prev = jnp.where(keep, pltpu.roll(a, d, 0), 1.0)
            u_prev = jnp.where(keep, pltpu.roll(u, d, 0), 0.0)
            u = a * u_prev + u
            a = a * a_prev
        a_ref[...] = a
        u_ref[...] = u

        def group_body(gi, h):
            g0 = pl.multiple_of(gi * SUBLANES, SUBLANES)
            hv = a_ref[pl.ds(g0, SUBLANES), :] * h + u_ref[pl.ds(g0, SUBLANES), :]
            u_ref[pl.ds(g0, SUBLANES), :] = hv
            return jnp.broadcast_to(hv[SUBLANES - 1:SUBLANES, :], (SUBLANES, c))

        h = lax.fori_loop(0, chunk // SUBLANES, group_body, h, unroll=8)
        o_ref[pl.ds(t0, chunk), :] = (
            u_ref[...] * y_ref[pl.ds(t0, chunk), :].astype(F32)).astype(o_ref.dtype)
        return h

    lax.fori_loop(0, s_len // chunk, chunk_body, jnp.zeros((SUBLANES, c), F32))


def _rglru(xb, yb, conv_w, conv_b, gate_w, gate_b, lru_lambda, *, batch, seq, chunk, name):
    width = xb.shape[-1]
    c = width // LRU_BLOCKS
    row = pl.BlockSpec((1, c), lambda b, n: (0, n))
    gate = lambda g: pl.BlockSpec((None, None, c, c), lambda b, n: (g, n, 0, 0))
    gate_bias = lambda g: pl.BlockSpec((None, 1, c), lambda b, n: (g, 0, n))
    return pl.pallas_call(
        functools.partial(_rglru_kernel, chunk=chunk),
        grid=(batch, LRU_BLOCKS),
        in_specs=[pl.BlockSpec((None, seq, c), lambda b, n: (b, 0, n)),
                  pl.BlockSpec((None, seq, c), lambda b, n: (b, 0, n)),
                  pl.BlockSpec((CONV_WIDTH, c), lambda b, n: (0, n)),
                  row, gate(0), gate(1), gate_bias(0), gate_bias(1), row],
        out_specs=pl.BlockSpec((None, seq, c), lambda b, n: (b, 0, n)),
        out_shape=jax.ShapeDtypeStruct((batch, seq, width), BF16),
        scratch_shapes=[pltpu.VMEM((seq + SUBLANES, c), F32),
                        pltpu.VMEM((chunk, c), F32),
                        pltpu.VMEM((chunk, c), F32)],
        compiler_params=_params(("parallel", "parallel"), 40),
        name=name,
    )(xb.reshape(batch, seq, width), yb.reshape(batch, seq, width), conv_w,
      conv_b.reshape(1, width), gate_w, gate_w, gate_b.reshape(2, 1, width),
      gate_b.reshape(2, 1, width), lru_lambda.reshape(1, width))


def _identity(v):
    return v


def _even_mixer(xf, g_pre, w_in, lq1, lk1, lq2, lk2, sub_g, sinks, w_out, g_post,
                *, batch, seq, lambda_init):
    m = batch * seq
    n_qk = 2 * DIFF_HEADS * HEAD_DIM
    n_v = DIFF_HEADS * 2 * HEAD_DIM
    n_sq = SWA_Q_HEADS * HEAD_DIM
    c0 = 2 * n_qk + n_v + n_sq
    ks = [w_in[:, c0 + h * HEAD_DIM:c0 + (h + 1) * HEAD_DIM] for h in range(SWA_KV_HEADS)]
    c1 = c0 + SWA_KV_HEADS * HEAD_DIM
    vs = [w_in[:, c1 + h * HEAD_DIM:c1 + (h + 1) * HEAD_DIM] for h in range(SWA_KV_HEADS)]
    w_main = jnp.concatenate(
        [w_in[:, :2 * n_qk], w_in[:, 2 * n_qk + n_v:c0]]
        + [t for k in ks for t in (k, k)] + [t for v in vs for t in (v, v)],
        axis=1).astype(BF16)
    w_v = w_in[:, 2 * n_qk:2 * n_qk + n_v].astype(BF16)

    tq = min(256, seq)
    tn = 512
    is_query = lambda c: c < n_qk or 2 * n_qk <= c < 2 * n_qk + n_sq
    main_scales = tuple(QK_SCALE_LOG2 if is_query(c) else 1.0
                        for c in range(0, w_main.shape[1], tn))
    proj, vt = _norm_proj(xf, g_pre, [w_main, w_v], [_identity, _identity], [BF16, BF16],
                          tm=min(512, m), tn=tn, col_scales=(main_scales, None),
                          transposed=(False, True), tk=tq, name="even_in_proj")
    proj3 = proj.reshape(batch, seq, proj.shape[1])
    a_out = _diff_attn(proj3, vt, lq1, lk1, lq2, lk2, sub_g, batch=batch, seq=seq, tq=tq,
                       lambda_init=lambda_init, name="diff_attn")
    swa_q_col = 2 * n_qk // (SWA_GROUP * HEAD_DIM)
    swa_k_col = (2 * n_qk + n_sq) // LANES
    b_out = _swa(proj3, sinks, batch=batch, seq=seq, q_col=swa_q_col, k_col=swa_k_col,
                 v_col=swa_k_col + SWA_KV_HEADS, name="swa")
    w_o = w_out.astype(BF16)
    return _out_proj([a_out.reshape(m, n_v), b_out.reshape(m, n_sq)],
                     [w_o[:n_v], w_o[n_v:]], xf, g_post, tm=min(512, m), sub=256,
                     name="even_out_proj")


def _odd_mixer(xf, g_pre, w_in, conv_w, conv_b, gate_w, gate_b, lru_lambda, w_out, g_post,
               *, batch, seq):
    m = batch * seq
    width = w_in.shape[1] // 2
    w_b = w_in.astype(BF16)
    yb, xb = _norm_proj(xf, g_pre, [w_b[:, :width], w_b[:, width:]],
                        [jax.nn.gelu, _identity], [BF16, F32],
                        tm=min(512, m), tn=512, name="odd_in_proj")
    hy = _rglru(xb, yb, conv_w, conv_b, gate_w.astype(BF16), gate_b, lru_lambda,
                batch=batch, seq=seq, chunk=min(512, seq), name="rglru")
    return _out_proj([hy.reshape(m, width)], [w_out.astype(BF16)], xf, g_post,
                     tm=min(512, m), sub=256, name="odd_out_proj")


def kernel(x, even_w_in, even_lam_q1, even_lam_k1, even_lam_q2, even_lam_k2, even_subln_g,
           even_sinks, even_w_out, odd_w_in, odd_conv_w, odd_conv_b, odd_gate_w, odd_gate_b,
           odd_lru_lambda, odd_w_out, pre_mix_g, post_mix_g, pre_mlp_g, post_mlp_g, mlp_w1,
           mlp_w2):
    batch, seq, d = x.shape
    m = batch * seq
    xf = x.reshape(m, d)
    depth = pre_mix_g.shape[0]
    for layer in range(depth):
        if layer % 2 == 0:
            e = layer // 2
            lambda_init = 0.8 - 0.6 * math.exp(-0.3 * layer)
            xf = _even_mixer(xf, pre_mix_g[layer], even_w_in[e], even_lam_q1[e],
                             even_lam_k1[e], even_lam_q2[e], even_lam_k2[e], even_subln_g[e],
                             even_sinks[e], even_w_out[e], post_mix_g[layer],
                             batch=batch, seq=seq, lambda_init=lambda_init)
        else:
            o = layer // 2
            xf = _odd_mixer(xf, pre_mix_g[layer], odd_w_in[o], odd_conv_w[o], odd_conv_b[o],
                            odd_gate_w[o], odd_gate_b[o], odd_lru_lambda[o], odd_w_out[o],
                            post_mix_g[layer], batch=batch, seq=seq)
        xf = _mlp(xf, pre_mlp_g[layer], mlp_w1[layer].astype(BF16), mlp_w2[layer].astype(BF16),
                  post_mlp_g[layer], tm=min(512, m), tf=1024, name=f"mlp_{layer}")
    return xf.reshape(batch, seq, d)
```

```python
import functools
import math

import jax
import jax.numpy as jnp
from jax import lax
from jax.experimental import pallas as pl
from jax.experimental.pallas import tpu as pltpu

F32 = jnp.float32
BF16 = jnp.bfloat16

EPS = 1e-6
HEAD_DIM = 64
DIFF_HEADS = 8
SWA_Q_HEADS = 16
SWA_KV_HEADS = 2
SWA_GROUP = SWA_Q_HEADS // SWA_KV_HEADS
SWA_BLOCK = 128
LRU_BLOCKS = 8
CONV_WIDTH = 4
LRU_C = 8.0

LANES = 128
SUBLANES = 8
MIB = 1024 * 1024

NT_DIMS = (((1,), (1,)), ((), ()))


def _params(semantics, vmem_mib):
    return pltpu.CompilerParams(dimension_semantics=semantics,
                                vmem_limit_bytes=vmem_mib * MIB)


def _rmsnorm(x, g):
    ms = jnp.mean(x * x, axis=-1, keepdims=True)
    return x * lax.rsqrt(ms + EPS) * g


def _resident(shape):
    return pl.BlockSpec(shape, lambda *_: (0,) * len(shape), pipeline_mode=pl.Buffered(1))


def _norm_proj_kernel(x_ref, g_ref, *refs, epilogues, col_scales, transposed, tn, tk):
    n = len(epilogues)
    w_refs, o_refs = refs[:n], refs[n:2 * n]
    x = x_ref[...]
    rs = lax.rsqrt(jnp.mean(x * x, axis=-1, keepdims=True) + EPS)
    xg = (x * g_ref[...]).astype(BF16)
    for w_ref, o_ref, epilogue, scales, tr in zip(w_refs, o_refs, epilogues, col_scales,
                                                  transposed):
        for c0 in range(0, w_ref.shape[1], tn):
            row_scale = rs if scales is None else rs * scales[c0 // tn]
            res = epilogue(
                jnp.dot(xg, w_ref[:, c0:c0 + tn], preferred_element_type=F32) * row_scale)
            if tr:
                res_t = res.T.astype(o_ref.dtype)
                for c in range(o_ref.shape[0]):
                    o_ref[c, c0:c0 + tn, :] = res_t[:, c * tk:(c + 1) * tk]
            else:
                o_ref[:, c0:c0 + tn] = res.astype(o_ref.dtype)


def _norm_proj(x, g, ws, epilogues, out_dtypes, *, tm, tn, name, col_scales=None,
               transposed=None, tk=None):
    m, d = x.shape
    transposed = transposed or (False,) * len(ws)
    col_scales = col_scales or (None,) * len(ws)
    out_specs, out_shape = [], []
    for w, dt, tr in zip(ws, out_dtypes, transposed):
        n = w.shape[1]
        if tr:
            out_specs.append(pl.BlockSpec((tm // tk, n, tk), lambda i: (i, 0, 0)))
            out_shape.append(jax.ShapeDtypeStruct((m // tk, n, tk), dt))
        else:
            out_specs.append(pl.BlockSpec((tm, n), lambda i: (i, 0)))
            out_shape.append(jax.ShapeDtypeStruct((m, n), dt))
    return pl.pallas_call(
        functools.partial(_norm_proj_kernel, epilogues=tuple(epilogues),
                          col_scales=tuple(col_scales), transposed=tuple(transposed),
                          tn=tn, tk=tk),
        grid=(m // tm,),
        in_specs=[pl.BlockSpec((tm, d), lambda i: (i, 0)), _resident((1, d))]
                 + [_resident(w.shape) for w in ws],
        out_specs=out_specs,
        out_shape=out_shape,
        compiler_params=_params(("parallel",), 52),
        name=name,
    )(x, g.reshape(1, d), *ws)


DIFF_HEADS_PER_STEP = 4
LOG2_E = 1.4426950408889634
QK_SCALE_LOG2 = HEAD_DIM ** -0.5 * LOG2_E
SUM_ROWS = 16


def _diff_attn_kernel(lq1_ref, lk1_ref, lq2_ref, lk2_ref, g_ref, q_ref, k_ref, vt_ref,
                      o_ref, *scratch, tq, lambda_init):
    hps = DIFF_HEADS_PER_STEP
    s_refs, qzt_refs, m_refs, acc_refs = (scratch[n * hps:(n + 1) * hps] for n in range(4))
    s_len = q_ref.shape[0]
    tk = tq
    heads = range(hps)
    ones_rows = jnp.ones((SUM_ROWS, tk), BF16)
    lam = (jnp.exp(jnp.sum(lq1_ref[...] * lk1_ref[...], keepdims=True))
           - jnp.exp(jnp.sum(lq2_ref[...] * lk2_ref[...], keepdims=True)) + lambda_init)
    first_map = lax.broadcasted_iota(jnp.int32, (LANES, tq), 0) < HEAD_DIM
    kpos = lax.broadcasted_iota(jnp.int32, (tk, 2 * tq), 0)
    qcol = lax.broadcasted_iota(jnp.int32, (tk, 2 * tq), 1)
    causal = kpos <= jnp.where(qcol >= tq, qcol - tq, qcol)

    def scores(hh, j):
        cols = slice(hh * LANES, (hh + 1) * LANES)
        k = k_ref[pl.ds(pl.multiple_of(j * tk, tk), tk), cols]
        s_refs[hh][...] = jnp.dot(k, qzt_refs[hh][...], preferred_element_type=F32)

    def consume(hh, j, masked):
        cols = slice(hh * LANES, (hh + 1) * LANES)
        st = s_refs[hh][...]
        if masked:
            st = jnp.where(causal, st, -jnp.inf)
        m_old = m_refs[hh][...]
        m_new = jnp.maximum(m_old, jnp.max(st, axis=0, keepdims=True))
        alpha = jnp.exp2(m_old - m_new)
        p = jnp.exp2(st - m_new).astype(BF16)
        vt_ones = jnp.concatenate([vt_ref[j, cols, :], ones_rows], axis=0)
        acc_refs[hh][...] = alpha * acc_refs[hh][...] + jnp.dot(
            vt_ones, p, preferred_element_type=F32)
        m_refs[hh][...] = m_new

    def start_q_tile(i):
        r0 = pl.multiple_of(i * tq, tq)
        for hh in heads:
            qt = q_ref[pl.ds(r0, tq), hh * LANES:(hh + 1) * LANES].astype(F32).T
            qzt_refs[hh][:, 0:tq] = jnp.where(first_map, qt, 0.0).astype(BF16)
            qzt_refs[hh][:, tq:2 * tq] = jnp.where(first_map, 0.0, qt).astype(BF16)
            scores(hh, 0)

    def reset_state():
        for hh in heads:
            m_refs[hh][...] = jnp.full(m_refs[hh].shape, -jnp.inf, F32)
            acc_refs[hh][...] = jnp.zeros(acc_refs[hh].shape, F32)

    def step(j):
        for hh in heads:
            consume(hh, j, False)
            scores(hh, j + 1)

    def finish_q_tile(i, odd):
        r0 = pl.multiple_of(i * tq, tq)
        if odd:
            step(i - 1)
        for hh in heads:
            consume(hh, i, True)
        start_q_tile(jnp.minimum(i + 1, n_q - 1))
        for hh in heads:
            acc = acc_refs[hh][0:LANES, :]
            inv_l = 1.0 / acc_refs[hh][LANES:LANES + 1, :]
            ot = acc[:, :tq] * inv_l[:, :tq] - lam * (acc[:, tq:] * inv_l[:, tq:])
            o = _rmsnorm(ot.T, g_ref[...]) * (1.0 - lambda_init)
            o_ref[pl.ds(r0, tq), hh * LANES:(hh + 1) * LANES] = o.astype(o_ref.dtype)
        reset_state()

    def q_body(i, carry):
        def pair(jp, c):
            step(2 * jp)
            step(2 * jp + 1)
            return c

        lax.fori_loop(0, i // 2, pair, 0)
        pl.when(i % 2 == 1)(lambda: finish_q_tile(i, True))
        pl.when(i % 2 == 0)(lambda: finish_q_tile(i, False))
        return carry

    n_q = s_len // tq
    reset_state()
    start_q_tile(0)
    lax.fori_loop(0, n_q, q_body, 0)


def _diff_attn(proj, vt, lq1, lk1, lq2, lk2, sub_g, *, batch, seq, tq, lambda_init, name):
    hps = DIFF_HEADS_PER_STEP
    width = hps * LANES
    groups = DIFF_HEADS // hps
    vec = pl.BlockSpec((1, HEAD_DIM), lambda b, h: (0, 0))
    return pl.pallas_call(
        functools.partial(_diff_attn_kernel, tq=tq, lambda_init=lambda_init),
        grid=(batch, groups),
        in_specs=[vec, vec, vec, vec,
                  pl.BlockSpec((1, LANES), lambda b, h: (0, 0)),
                  pl.BlockSpec((None, seq, width), lambda b, h: (b, 0, h)),
                  pl.BlockSpec((None, seq, width), lambda b, h: (b, 0, groups + h)),
                  pl.BlockSpec((seq // tq, width, tq), lambda b, h: (b, h, 0))],
        out_specs=pl.BlockSpec((None, seq, width), lambda b, h: (b, 0, h)),
        out_shape=jax.ShapeDtypeStruct((batch, seq, DIFF_HEADS * LANES), BF16),
        scratch_shapes=[pltpu.VMEM((tq, 2 * tq), F32)] * hps
                       + [pltpu.VMEM((LANES, 2 * tq), BF16)] * hps
                       + [pltpu.VMEM((1, 2 * tq), F32)] * hps
                       + [pltpu.VMEM((LANES + SUM_ROWS, 2 * tq), F32)] * hps,
        compiler_params=_params(("parallel", "parallel"), 48),
        name=name,
    )(lq1.reshape(1, -1), lk1.reshape(1, -1), lq2.reshape(1, -1), lk2.reshape(1, -1),
      sub_g.reshape(1, -1), proj, proj, vt)


SWA_BLOCKS_PER_TRIP = 4


def _swa_kernel(sinks_ref, q_ref, kk_ref, vv_ref, o_ref):
    s_len = q_ref.shape[0]
    blk = SWA_BLOCK
    kvh = pl.program_id(1)
    first_head = lax.broadcasted_iota(jnp.int32, (blk, LANES), 1) < HEAD_DIM
    sinks = [sinks_ref[kvh * SWA_GROUP + g] * LOG2_E for g in range(SWA_GROUP)]

    nkeys = 2 * blk
    col_minus_row = (lax.broadcasted_iota(jnp.int32, (blk, nkeys), 1)
                     - lax.broadcasted_iota(jnp.int32, (blk, nkeys), 0))

    def block_scores(n):
        r0 = pl.multiple_of(n * blk, blk)
        k0 = pl.multiple_of(jnp.maximum(n - 1, 0) * blk, blk)
        qparts = []
        for gp in range(SWA_GROUP // 2):
            qq = q_ref[pl.ds(r0, blk), gp * LANES:(gp + 1) * LANES].astype(F32)
            qparts.append(jnp.where(first_head, qq, 0.0).astype(BF16))
            qparts.append(jnp.where(first_head, 0.0, qq).astype(BF16))
        qs = jnp.concatenate(qparts, axis=0)
        kk = kk_ref[pl.ds(k0, nkeys), :]
        return lax.dot_general(qs, kk, NT_DIMS, preferred_element_type=F32)

    def block_finish(n, s):
        r0 = pl.multiple_of(n * blk, blk)
        k0 = pl.multiple_of(jnp.maximum(n - 1, 0) * blk, blk)
        back = (r0 - k0) - col_minus_row
        valid = jnp.logical_and(back >= 0, back < blk)
        ps = []
        for g in range(SWA_GROUP):
            sg = jnp.where(valid, s[g * blk:(g + 1) * blk], -jnp.inf)
            m = jnp.maximum(jnp.max(sg, axis=-1, keepdims=True), sinks[g])
            e = jnp.exp2(sg - m)
            denom = jnp.sum(e, axis=-1, keepdims=True) + jnp.exp2(sinks[g] - m)
            ps.append((e * (1.0 / denom)).astype(BF16))
        p = jnp.concatenate(ps, axis=0)
        o = jnp.dot(p, vv_ref[pl.ds(k0, nkeys), :], preferred_element_type=F32)
        for gp in range(SWA_GROUP // 2):
            lo = o[(2 * gp) * blk:(2 * gp + 1) * blk]
            hi = o[(2 * gp + 1) * blk:(2 * gp + 2) * blk]
            o_ref[pl.ds(r0, blk), gp * LANES:(gp + 1) * LANES] = jnp.where(
                first_head, lo, hi).astype(o_ref.dtype)

    def body(gi, c):
        blocks = [gi * SWA_BLOCKS_PER_TRIP + u for u in range(SWA_BLOCKS_PER_TRIP)]
        scores = [block_scores(n) for n in blocks]
        for n, s in zip(blocks, scores):
            block_finish(n, s)
        return c

    lax.fori_loop(0, s_len // (blk * SWA_BLOCKS_PER_TRIP), body, 0)


def _swa(proj, sinks, *, batch, seq, q_col, k_col, v_col, name):
    qw = SWA_GROUP * HEAD_DIM
    return pl.pallas_call(
        _swa_kernel,
        grid=(batch, SWA_KV_HEADS),
        in_specs=[pl.BlockSpec(memory_space=pltpu.SMEM),
                  pl.BlockSpec((None, seq, qw), lambda b, h: (b, 0, q_col + h)),
                  pl.BlockSpec((None, seq, LANES), lambda b, h: (b, 0, k_col + h)),
                  pl.BlockSpec((None, seq, LANES), lambda b, h: (b, 0, v_col + h))],
        out_specs=pl.BlockSpec((None, seq, qw), lambda b, h: (b, 0, h)),
        out_shape=jax.ShapeDtypeStruct((batch, seq, SWA_Q_HEADS * HEAD_DIM), BF16),
        compiler_params=_params(("parallel", "parallel"), 40),
        name=name,
    )(sinks, proj, proj, proj)


def _out_proj_kernel(*refs, n_in, sub):
    a_refs, w_refs = refs[:n_in], refs[n_in:2 * n_in]
    x_ref, g_ref, o_ref = refs[2 * n_in:]
    for r0 in range(0, x_ref.shape[0], sub):
        rows = slice(r0, r0 + sub)
        mix = jnp.dot(a_refs[0][rows, :], w_refs[0][...], preferred_element_type=F32)
        for a_ref, w_ref in zip(a_refs[1:], w_refs[1:]):
            mix = mix + jnp.dot(a_ref[rows, :], w_ref[...], preferred_element_type=F32)
        o_ref[rows, :] = x_ref[rows, :] + _rmsnorm(mix, g_ref[...])


def _out_proj(acts, ws, x, g, *, tm, sub, name):
    m, d = x.shape
    n_in = len(acts)
    return pl.pallas_call(
        functools.partial(_out_proj_kernel, n_in=n_in, sub=sub),
        grid=(m // tm,),
        in_specs=[pl.BlockSpec((tm, a.shape[1]), lambda i: (i, 0)) for a in acts]
                 + [_resident(w.shape) for w in ws]
                 + [pl.BlockSpec((tm, d), lambda i: (i, 0)), _resident((1, d))],
        out_specs=pl.BlockSpec((tm, d), lambda i: (i, 0)),
        out_shape=jax.ShapeDtypeStruct((m, d), F32),
        compiler_params=_params(("parallel",), 52),
        name=name,
    )(*acts, *ws, x, g.reshape(1, d))


def _mlp_kernel(x_ref, g1_ref, w1_ref, w2_ref, g2_ref, o_ref, hn_ref, acc_ref):
    j = pl.program_id(1)

    @pl.when(j == 0)
    def _():
        hn_ref[...] = _rmsnorm(x_ref[...], g1_ref[...]).astype(BF16)
        acc_ref[...] = jnp.zeros(acc_ref.shape, F32)

    z = jnp.maximum(jnp.dot(hn_ref[...], w1_ref[...], preferred_element_type=F32), 0.0)
    acc_ref[...] += jnp.dot((z * z).astype(BF16), w2_ref[...], preferred_element_type=F32)

    @pl.when(j == pl.num_programs(1) - 1)
    def _():
        o_ref[...] = x_ref[...] + _rmsnorm(acc_ref[...], g2_ref[...])


def _mlp(x, g1, w1, w2, g2, *, tm, tf, name):
    m, d = x.shape
    f = w1.shape[1]
    return pl.pallas_call(
        _mlp_kernel,
        grid=(m // tm, f // tf),
        in_specs=[pl.BlockSpec((tm, d), lambda i, j: (i, 0)),
                  _resident((1, d)),
                  pl.BlockSpec((d, tf), lambda i, j: (0, j)),
                  pl.BlockSpec((tf, d), lambda i, j: (j, 0)),
                  _resident((1, d))],
        out_specs=pl.BlockSpec((tm, d), lambda i, j: (i, 0)),
        out_shape=jax.ShapeDtypeStruct((m, d), F32),
        scratch_shapes=[pltpu.VMEM((tm, d), BF16), pltpu.VMEM((tm, d), F32)],
        compiler_params=_params(("parallel", "arbitrary"), 52),
        name=name,
    )(x, g1.reshape(1, d), w1, w2, g2.reshape(1, d))


def _rglru_kernel(x_ref, y_ref, cw_ref, cb_ref, wr_ref, wi_ref, br_ref, bi_ref, lam_ref,
                  o_ref, xp_ref, a_ref, u_ref, *, chunk):
    s_len, c = x_ref.shape
    pad = SUBLANES
    xp_ref[0:pad, :] = jnp.zeros((pad, c), F32)
    xp_ref[pad:, :] = x_ref[...]
    softplus_neg = jax.nn.softplus(-lam_ref[...])
    cw = cw_ref[...]
    groups = chunk // SUBLANES
    row_in_group = lax.broadcasted_iota(jnp.int32, (groups, SUBLANES, c), 1)

    def sigmoid(v):
        return 0.5 * jnp.tanh(0.5 * v) + 0.5

    def chunk_body(ci, h):
        t0 = ci * chunk
        xc = cb_ref[...] + cw[CONV_WIDTH - 1:CONV_WIDTH] * xp_ref[t0 + pad:t0 + pad + chunk, :]
        for d in range(1, CONV_WIDTH):
            tap = cw[CONV_WIDTH - 1 - d:CONV_WIDTH - d]
            xc = xc + tap * xp_ref[t0 + pad - d:t0 + pad - d + chunk, :]
        xcb = xc.astype(BF16)
        r = sigmoid(jnp.dot(xcb, wr_ref[...], preferred_element_type=F32) + br_ref[...])
        ig = sigmoid(jnp.dot(xcb, wi_ref[...], preferred_element_type=F32) + bi_ref[...])
        a = jnp.exp(-LRU_C * r * softplus_neg)
        y = 1.0 - a * a
        u = jnp.where(y > 0.0, y * lax.rsqrt(y), 0.0) * (ig * xc)
        a = a.reshape(groups, SUBLANES, c)
        u = u.reshape(groups, SUBLANES, c)
        for d in (1, 2, 4):
            keep = row_in_group >= d
            a_prev = jnp.where(keep, pltpu.roll(a, d, 1), 1.0)
            u_prev = jnp.where(keep, pltpu.roll(u, d, 1), 0.0)
            u = a * u_prev + u
            a = a * a_prev
        a_ref[...] = a.reshape(chunk, c)
        u_ref[...] = u.reshape(chunk, c)

        def group_body(gi, h):
            g0 = pl.multiple_of(gi * SUBLANES, SUBLANES)
            hv = a_ref[pl.ds(g0, SUBLANES), :] * h + u_ref[pl.ds(g0, SUBLANES), :]
            u_ref[pl.ds(g0, SUBLANES), :] = hv
            return jnp.broadcast_to(hv[SUBLANES - 1:SUBLANES, :], (SUBLANES, c))

        h = lax.fori_loop(0, groups, group_body, h, unroll=8)
        o_ref[t0:t0 + chunk, :] = (
            u_ref[...] * y_ref[t0:t0 + chunk, :].astype(F32)).astype(o_ref.dtype)
        return h

    h = jnp.zeros((SUBLANES, c), F32)
    for ci in range(s_len // chunk):
        h = chunk_body(ci, h)


def _rglru(xb, yb, conv_w, conv_b, gate_w, gate_b, lru_lambda, *, batch, seq, chunk, name):
    width = xb.shape[-1]
    c = width // LRU_BLOCKS
    row = pl.BlockSpec((1, c), lambda b, n: (0, n))
    gate = lambda g: pl.BlockSpec((None, None, c, c), lambda b, n: (g, n, 0, 0))
    gate_bias = lambda g: pl.BlockSpec((None, 1, c), lambda b, n: (g, 0, n))
    return pl.pallas_call(
        functools.partial(_rglru_kernel, chunk=chunk),
        grid=(batch, LRU_BLOCKS),
        in_specs=[pl.BlockSpec((None, seq, c), lambda b, n: (b, 0, n)),
                  pl.BlockSpec((None, seq, c), lambda b, n: (b, 0, n)),
                  pl.BlockSpec((CONV_WIDTH, c), lambda b, n: (0, n)),
                  row, gate(0), gate(1), gate_bias(0), gate_bias(1), row],
        out_specs=pl.BlockSpec((None, seq, c), lambda b, n: (b, 0, n)),
        out_shape=jax.ShapeDtypeStruct((batch, seq, width), BF16),
        scratch_shapes=[pltpu.VMEM((seq + SUBLANES, c), F32),
                        pltpu.VMEM((chunk, c), F32),
                        pltpu.VMEM((chunk, c), F32)],
        compiler_params=_params(("parallel", "parallel"), 40),
        name=name,
    )(xb.reshape(batch, seq, width), yb.reshape(batch, seq, width), conv_w,
      conv_b.reshape(1, width), gate_w, gate_w, gate_b.reshape(2, 1, width),
      gate_b.reshape(2, 1, width), lru_lambda.reshape(1, width))


def _identity(v):
    return v


def _even_mixer(xf, g_pre, w_in, lq1, lk1, lq2, lk2, sub_g, sinks, w_out, g_post,
                *, batch, seq, lambda_init):
    m = batch * seq
    n_qk = 2 * DIFF_HEADS * HEAD_DIM
    n_v = DIFF_HEADS * 2 * HEAD_DIM
    n_sq = SWA_Q_HEADS * HEAD_DIM
    c0 = 2 * n_qk + n_v + n_sq
    ks = [w_in[:, c0 + h * HEAD_DIM:c0 + (h + 1) * HEAD_DIM] for h in range(SWA_KV_HEADS)]
    c1 = c0 + SWA_KV_HEADS * HEAD_DIM
    vs = [w_in[:, c1 + h * HEAD_DIM:c1 + (h + 1) * HEAD_DIM] for h in range(SWA_KV_HEADS)]
    w_main = jnp.concatenate(
        [w_in[:, :2 * n_qk], w_in[:, 2 * n_qk + n_v:c0]]
        + [t for k in ks for t in (k, k)] + [t for v in vs for t in (v, v)],
        axis=1).astype(BF16)
    w_v = w_in[:, 2 * n_qk:2 * n_qk + n_v].astype(BF16)

    tq = min(256, seq)
    tn = 512
    is_query = lambda c: c < n_qk or 2 * n_qk <= c < 2 * n_qk + n_sq
    main_scales = tuple(QK_SCALE_LOG2 if is_query(c) else 1.0
                        for c in range(0, w_main.shape[1], tn))
    proj, vt = _norm_proj(xf, g_pre, [w_main, w_v], [_identity, _identity], [BF16, BF16],
                          tm=min(512, m), tn=tn, col_scales=(main_scales, None),
                          transposed=(False, True), tk=tq, name="even_in_proj")
    proj3 = proj.reshape(batch, seq, proj.shape[1])
    a_out = _diff_attn(proj3, vt, lq1, lk1, lq2, lk2, sub_g, batch=batch, seq=seq, tq=tq,
                       lambda_init=lambda_init, name="diff_attn")
    swa_q_col = 2 * n_qk // (SWA_GROUP * HEAD_DIM)
    swa_k_col = (2 * n_qk + n_sq) // LANES
    b_out = _swa(proj3, sinks, batch=batch, seq=seq, q_col=swa_q_col, k_col=swa_k_col,
                 v_col=swa_k_col + SWA_KV_HEADS, name="swa")
    w_o = w_out.astype(BF16)
    return _out_proj([a_out.reshape(m, n_v), b_out.reshape(m, n_sq)],
                     [w_o[:n_v], w_o[n_v:]], xf, g_post, tm=min(512, m), sub=256,
                     name="even_out_proj")


def _odd_mixer(xf, g_pre, w_in, conv_w, conv_b, gate_w, gate_b, lru_lambda, w_out, g_post,
               *, batch, seq):
    m = batch * seq
    width = w_in.shape[1] // 2
    w_b = w_in.astype(BF16)
    yb, xb = _norm_proj(xf, g_pre, [w_b[:, :width], w_b[:, width:]],
                        [jax.nn.gelu, _identity], [BF16, F32],
                        tm=min(512, m), tn=512, name="odd_in_proj")
    hy = _rglru(xb, yb, conv_w, conv_b, gate_w.astype(BF16), gate_b, lru_lambda,
                batch=batch, seq=seq, chunk=min(512, seq), name="rglru")
    return _out_proj([hy.reshape(m, width)], [w_out.astype(BF16)], xf, g_post,
                     tm=min(512, m), sub=256, name="odd_out_proj")


def kernel(x, even_w_in, even_lam_q1, even_lam_k1, even_lam_q2, even_lam_k2, even_subln_g,
           even_sinks, even_w_out, odd_w_in, odd_conv_w, odd_conv_b, odd_gate_w, odd_gate_b,
           odd_lru_lambda, odd_w_out, pre_mix_g, post_mix_g, pre_mlp_g, post_mlp_g, mlp_w1,
           mlp_w2):
    batch, seq, d = x.shape
    m = batch * seq
    xf = x.reshape(m, d)
    depth = pre_mix_g.shape[0]
    for layer in range(depth):
        if layer % 2 == 0:
            e = layer // 2
            lambda_init = 0.8 - 0.6 * math.exp(-0.3 * layer)
            xf = _even_mixer(xf, pre_mix_g[layer], even_w_in[e], even_lam_q1[e],
                             even_lam_k1[e], even_lam_q2[e], even_lam_k2[e], even_subln_g[e],
                             even_sinks[e], even_w_out[e], post_mix_g[layer],
                             batch=batch, seq=seq, lambda_init=lambda_init)
        else:
            o = layer // 2
            xf = _odd_mixer(xf, pre_mix_g[layer], odd_w_in[o], odd_conv_w[o], odd_conv_b[o],
                            odd_gate_w[o], odd_gate_b[o], odd_lru_lambda[o], odd_w_out[o],
                            post_mix_g[layer], batch=batch, seq=seq)
        xf = _mlp(xf, pre_mlp_g[layer], mlp_w1[layer].astype(BF16), mlp_w2[layer].astype(BF16),
                  post_mlp_g[layer], tm=min(512, m), tf=1024, name=f"mlp_{layer}")
    return xf.reshape(batch, seq, d)
```

```python
import functools
import math

import jax
import jax.numpy as jnp
from jax import lax
from jax.experimental import pallas as pl
from jax.experimental.pallas import tpu as pltpu

F32 = jnp.float32
BF16 = jnp.bfloat16

EPS = 1e-6
HEAD_DIM = 64
DIFF_HEADS = 8
SWA_Q_HEADS = 16
SWA_KV_HEADS = 2
SWA_GROUP = SWA_Q_HEADS // SWA_KV_HEADS
SWA_BLOCK = 128
LRU_BLOCKS = 8
CONV_WIDTH = 4
LRU_C = 8.0

LANES = 128
SUBLANES = 8
MIB = 1024 * 1024

NT_DIMS = (((1,), (1,)), ((), ()))


def _params(semantics, vmem_mib):
    return pltpu.CompilerParams(dimension_semantics=semantics,
                                vmem_limit_bytes=vmem_mib * MIB)


def _rmsnorm(x, g):
    ms = jnp.mean(x * x, axis=-1, keepdims=True)
    return x * lax.rsqrt(ms + EPS) * g


def _resident(shape):
    return pl.BlockSpec(shape, lambda *_: (0,) * len(shape), pipeline_mode=pl.Buffered(1))


def _norm_proj_kernel(x_ref, g_ref, *refs, epilogues, col_scales, transposed, tn, tk):
    n = len(epilogues)
    w_refs, o_refs = refs[:n], refs[n:2 * n]
    x = x_ref[...]
    rs = lax.rsqrt(jnp.mean(x * x, axis=-1, keepdims=True) + EPS)
    xg = (x * g_ref[...]).astype(BF16)
    for w_ref, o_ref, epilogue, scales, tr in zip(w_refs, o_refs, epilogues, col_scales,
                                                  transposed):
        for c0 in range(0, w_ref.shape[1], tn):
            row_scale = rs if scales is None else rs * scales[c0 // tn]
            res = epilogue(
                jnp.dot(xg, w_ref[:, c0:c0 + tn], preferred_element_type=F32) * row_scale)
            if tr:
                res_t = res.T.astype(o_ref.dtype)
                for c in range(o_ref.shape[0]):
                    o_ref[c, c0:c0 + tn, :] = res_t[:, c * tk:(c + 1) * tk]
            else:
                o_ref[:, c0:c0 + tn] = res.astype(o_ref.dtype)


def _norm_proj(x, g, ws, epilogues, out_dtypes, *, tm, tn, name, col_scales=None,
               transposed=None, tk=None):
    m, d = x.shape
    transposed = transposed or (False,) * len(ws)
    col_scales = col_scales or (None,) * len(ws)
    out_specs, out_shape = [], []
    for w, dt, tr in zip(ws, out_dtypes, transposed):
        n = w.shape[1]
        if tr:
            out_specs.append(pl.BlockSpec((tm // tk, n, tk), lambda i: (i, 0, 0)))
            out_shape.append(jax.ShapeDtypeStruct((m // tk, n, tk), dt))
        else:
            out_specs.append(pl.BlockSpec((tm, n), lambda i: (i, 0)))
            out_shape.append(jax.ShapeDtypeStruct((m, n), dt))
    return pl.pallas_call(
        functools.partial(_norm_proj_kernel, epilogues=tuple(epilogues),
                          col_scales=tuple(col_scales), transposed=tuple(transposed),
                          tn=tn, tk=tk),
        grid=(m // tm,),
        in_specs=[pl.BlockSpec((tm, d), lambda i: (i, 0)), _resident((1, d))]
                 + [_resident(w.shape) for w in ws],
        out_specs=out_specs,
        out_shape=out_shape,
        compiler_params=_params(("parallel",), 52),
        name=name,
    )(x, g.reshape(1, d), *ws)


DIFF_HEADS_PER_STEP = 4
LOG2_E = 1.4426950408889634
QK_SCALE_LOG2 = HEAD_DIM ** -0.5 * LOG2_E
SUM_ROWS = 16


def _diff_attn_kernel(lq1_ref, lk1_ref, lq2_ref, lk2_ref, g_ref, q_ref, k_ref, vt_ref,
                      o_ref, *scratch, tq, lambda_init):
    hps = DIFF_HEADS_PER_STEP
    s_refs, qzt_refs, m_refs, acc_refs = (scratch[n * hps:(n + 1) * hps] for n in range(4))
    s_len = q_ref.shape[0]
    tk = tq
    heads = range(hps)
    ones_rows = jnp.ones((SUM_ROWS, tk), BF16)
    lam = (jnp.exp(jnp.sum(lq1_ref[...] * lk1_ref[...], keepdims=True))
           - jnp.exp(jnp.sum(lq2_ref[...] * lk2_ref[...], keepdims=True)) + lambda_init)
    first_map = lax.broadcasted_iota(jnp.int32, (LANES, tq), 0) < HEAD_DIM
    kpos = lax.broadcasted_iota(jnp.int32, (tk, 2 * tq), 0)
    qcol = lax.broadcasted_iota(jnp.int32, (tk, 2 * tq), 1)
    causal = kpos <= jnp.where(qcol >= tq, qcol - tq, qcol)

    def scores(hh, j):
        cols = slice(hh * LANES, (hh + 1) * LANES)
        k = k_ref[pl.ds(pl.multiple_of(j * tk, tk), tk), cols]
        s_refs[hh][...] = jnp.dot(k, qzt_refs[hh][...], preferred_element_type=F32)

    def consume(hh, j, masked):
        cols = slice(hh * LANES, (hh + 1) * LANES)
        st = s_refs[hh][...]
        if masked:
            st = jnp.where(causal, st, -jnp.inf)
        m_old = m_refs[hh][...]
        m_new = jnp.maximum(m_old, jnp.max(st, axis=0, keepdims=True))
        alpha = jnp.exp2(m_old - m_new)
        p = jnp.exp2(st - m_new).astype(BF16)
        vt_ones = jnp.concatenate([vt_ref[j, cols, :], ones_rows], axis=0)
        acc_refs[hh][...] = alpha * acc_refs[hh][...] + jnp.dot(
            vt_ones, p, preferred_element_type=F32)
        m_refs[hh][...] = m_new

    def start_q_tile(i):
        r0 = pl.multiple_of(i * tq, tq)
        for hh in heads:
            qt = q_ref[pl.ds(r0, tq), hh * LANES:(hh + 1) * LANES].astype(F32).T
            qzt_refs[hh][:, 0:tq] = jnp.where(first_map, qt, 0.0).astype(BF16)
            qzt_refs[hh][:, tq:2 * tq] = jnp.where(first_map, 0.0, qt).astype(BF16)
            scores(hh, 0)

    def reset_state():
        for hh in heads:
            m_refs[hh][...] = jnp.full(m_refs[hh].shape, -jnp.inf, F32)
            acc_refs[hh][...] = jnp.zeros(acc_refs[hh].shape, F32)

    def step(j):
        for hh in heads:
            consume(hh, j, False)
            scores(hh, j + 1)

    def finish_q_tile(i, odd):
        r0 = pl.multiple_of(i * tq, tq)
        if odd:
            step(i - 1)
        for hh in heads:
            consume(hh, i, True)
        start_q_tile(jnp.minimum(i + 1, n_q - 1))
        for hh in heads:
            acc = acc_refs[hh][0:LANES, :]
            inv_l = 1.0 / acc_refs[hh][LANES:LANES + 1, :]
            ot = acc[:, :tq] * inv_l[:, :tq] - lam * (acc[:, tq:] * inv_l[:, tq:])
            o = _rmsnorm(ot.T, g_ref[...]) * (1.0 - lambda_init)
            o_ref[pl.ds(r0, tq), hh * LANES:(hh + 1) * LANES] = o.astype(o_ref.dtype)
        reset_state()

    def q_body(i, carry):
        def pair(jp, c):
            step(2 * jp)
            step(2 * jp + 1)
            return c

        lax.fori_loop(0, i // 2, pair, 0)
        pl.when(i % 2 == 1)(lambda: finish_q_tile(i, True))
        pl.when(i % 2 == 0)(lambda: finish_q_tile(i, False))
        return carry

    n_q = s_len // tq
    reset_state()
    start_q_tile(0)
    lax.fori_loop(0, n_q, q_body, 0)


def _diff_attn(proj, vt, lq1, lk1, lq2, lk2, sub_g, *, batch, seq, tq, lambda_init, name):
    hps = DIFF_HEADS_PER_STEP
    width = hps * LANES
    groups = DIFF_HEADS // hps
    vec = pl.BlockSpec((1, HEAD_DIM), lambda b, h: (0, 0))
    return pl.pallas_call(
        functools.partial(_diff_attn_kernel, tq=tq, lambda_init=lambda_init),
        grid=(batch, groups),
        in_specs=[vec, vec, vec, vec,
                  pl.BlockSpec((1, LANES), lambda b, h: (0, 0)),
                  pl.BlockSpec((None, seq, width), lambda b, h: (b, 0, h)),
                  pl.BlockSpec((None, seq, width), lambda b, h: (b, 0, groups + h)),
                  pl.BlockSpec((seq // tq, width, tq), lambda b, h: (b, h, 0))],
        out_specs=pl.BlockSpec((None, seq, width), lambda b, h: (b, 0, h)),
        out_shape=jax.ShapeDtypeStruct((batch, seq, DIFF_HEADS * LANES), BF16),
        scratch_shapes=[pltpu.VMEM((tq, 2 * tq), F32)] * hps
                       + [pltpu.VMEM((LANES, 2 * tq), BF16)] * hps
                       + [pltpu.VMEM((1, 2 * tq), F32)] * hps
                       + [pltpu.VMEM((LANES + SUM_ROWS, 2 * tq), F32)] * hps,
        compiler_params=_params(("parallel", "parallel"), 48),
        name=name,
    )(lq1.reshape(1, -1), lk1.reshape(1, -1), lq2.reshape(1, -1), lk2.reshape(1, -1),
      sub_g.reshape(1, -1), proj, proj, vt)


SWA_BLOCKS_PER_TRIP = 4


def _swa_kernel(sinks_ref, q_ref, kk_ref, vv_ref, o_ref):
    s_len = q_ref.shape[0]
    blk = SWA_BLOCK
    kvh = pl.program_id(1)
    first_head = lax.broadcasted_iota(jnp.int32, (blk, LANES), 1) < HEAD_DIM
    sinks = [sinks_ref[kvh * SWA_GROUP + g] * LOG2_E for g in range(SWA_GROUP)]

    nkeys = 2 * blk
    col_minus_row = (lax.broadcasted_iota(jnp.int32, (blk, nkeys), 1)
                     - lax.broadcasted_iota(jnp.int32, (blk, nkeys), 0))

    def block_scores(n):
        r0 = pl.multiple_of(n * blk, blk)
        k0 = pl.multiple_of(jnp.maximum(n - 1, 0) * blk, blk)
        qparts = []
        for gp in range(SWA_GROUP // 2):
            qq = q_ref[pl.ds(r0, blk), gp * LANES:(gp + 1) * LANES].astype(F32)
            qparts.append(jnp.where(first_head, qq, 0.0).astype(BF16))
            qparts.append(jnp.where(first_head, 0.0, qq).astype(BF16))
        qs = jnp.concatenate(qparts, axis=0)
        kk = kk_ref[pl.ds(k0, nkeys), :]
        return lax.dot_general(qs, kk, NT_DIMS, preferred_element_type=F32)

    def block_finish(n, s):
        r0 = pl.multiple_of(n * blk, blk)
        k0 = pl.multiple_of(jnp.maximum(n - 1, 0) * blk, blk)
        back = (r0 - k0) - col_minus_row
        valid = jnp.logical_and(back >= 0, back < blk)
        ps = []
        for g in range(SWA_GROUP):
            sg = jnp.where(valid, s[g * blk:(g + 1) * blk], -jnp.inf)
            m = jnp.maximum(jnp.max(sg, axis=-1, keepdims=True), sinks[g])
            e = jnp.exp2(sg - m)
            denom = jnp.sum(e, axis=-1, keepdims=True) + jnp.exp2(sinks[g] - m)
            ps.append((e * (1.0 / denom)).astype(BF16))
        p = jnp.concatenate(ps, axis=0)
        o = jnp.dot(p, vv_ref[pl.ds(k0, nkeys), :], preferred_element_type=F32)
        for gp in range(SWA_GROUP // 2):
            lo = o[(2 * gp) * blk:(2 * gp + 1) * blk]
            hi = o[(2 * gp + 1) * blk:(2 * gp + 2) * blk]
            o_ref[pl.ds(r0, blk), gp * LANES:(gp + 1) * LANES] = jnp.where(
                first_head, lo, hi).astype(o_ref.dtype)

    def body(gi, c):
        blocks = [gi * SWA_BLOCKS_PER_TRIP + u for u in range(SWA_BLOCKS_PER_TRIP)]
        scores = [block_scores(n) for n in blocks]
        for n, s in zip(blocks, scores):
            block_finish(n, s)
        return c

    lax.fori_loop(0, s_len // (blk * SWA_BLOCKS_PER_TRIP), body, 0)


def _swa(proj, sinks, *, batch, seq, q_col, k_col, v_col, name):
    qw = SWA_GROUP * HEAD_DIM
    return pl.pallas_call(
        _swa_kernel,
        grid=(batch, SWA_KV_HEADS),
        in_specs=[pl.BlockSpec(memory_space=pltpu.SMEM),
                  pl.BlockSpec((None, seq, qw), lambda b, h: (b, 0, q_col + h)),
                  pl.BlockSpec((None, seq, LANES), lambda b, h: (b, 0, k_col + h)),
                  pl.BlockSpec((None, seq, LANES), lambda b, h: (b, 0, v_col + h))],
        out_specs=pl.BlockSpec((None, seq, qw), lambda b, h: (b, 0, h)),
        out_shape=jax.ShapeDtypeStruct((batch, seq, SWA_Q_HEADS * HEAD_DIM), BF16),
        compiler_params=_params(("parallel", "parallel"), 40),
        name=name,
    )(sinks, proj, proj, proj)


def _out_proj_kernel(*refs, n_in, sub):
    a_refs, w_refs = refs[:n_in], refs[n_in:2 * n_in]
    x_ref, g_ref, o_ref = refs[2 * n_in:]
    for r0 in range(0, x_ref.shape[0], sub):
        rows = slice(r0, r0 + sub)
        mix = jnp.dot(a_refs[0][rows, :], w_refs[0][...], preferred_element_type=F32)
        for a_ref, w_ref in zip(a_refs[1:], w_refs[1:]):
            mix = mix + jnp.dot(a_ref[rows, :], w_ref[...], preferred_element_type=F32)
        o_ref[rows, :] = x_ref[rows, :] + _rmsnorm(mix, g_ref[...])


def _out_proj(acts, ws, x, g, *, tm, sub, name):
    m, d = x.shape
    n_in = len(acts)
    return pl.pallas_call(
        functools.partial(_out_proj_kernel, n_in=n_in, sub=sub),
        grid=(m // tm,),
        in_specs=[pl.BlockSpec((tm, a.shape[1]), lambda i: (i, 0)) for a in acts]
                 + [_resident(w.shape) for w in ws]
                 + [pl.BlockSpec((tm, d), lambda i: (i, 0)), _resident((1, d))],
        out_specs=pl.BlockSpec((tm, d), lambda i: (i, 0)),
        out_shape=jax.ShapeDtypeStruct((m, d), F32),
        compiler_params=_params(("parallel",), 52),
        name=name,
    )(*acts, *ws, x, g.reshape(1, d))


def _mlp_kernel(x_ref, g1_ref, w1_ref, w2_ref, g2_ref, o_ref, hn_ref, acc_ref, *, sub):
    j = pl.program_id(1)
    last = pl.num_programs(1) - 1
    tm = x_ref.shape[0]

    def step(first, final, rows_per_pass):
        for r0 in range(0, tm, rows_per_pass):
            rows = slice(r0, r0 + rows_per_pass)
            if first:
                hn = _rmsnorm(x_ref[rows, :], g1_ref[...]).astype(BF16)
                hn_ref[rows, :] = hn
            else:
                hn = hn_ref[rows, :]
            z = jnp.maximum(jnp.dot(hn, w1_ref[...], preferred_element_type=F32), 0.0)
            part = jnp.dot((z * z).astype(BF16), w2_ref[...], preferred_element_type=F32)
            acc = part if first else acc_ref[rows, :] + part
            if final:
                o_ref[rows, :] = x_ref[rows, :] + _rmsnorm(acc, g2_ref[...])
            else:
                acc_ref[rows, :] = acc

    pl.when(j == 0)(lambda: step(True, False, sub))
    pl.when(jnp.logical_and(j > 0, j < last))(lambda: step(False, False, tm))
    pl.when(j == last)(lambda: step(False, True, sub))


def _mlp(x, g1, w1, w2, g2, *, layer, tm, tf, sub, name):
    m, d = x.shape
    f = w1.shape[2]
    assert f // tf >= 2, "the first and the last d_ff step must be different steps"
    return pl.pallas_call(
        functools.partial(_mlp_kernel, sub=sub),
        grid=(m // tm, f // tf),
        in_specs=[pl.BlockSpec((tm, d), lambda i, j: (i, 0)),
                  _resident((1, d)),
                  pl.BlockSpec((None, d, tf), lambda i, j: (layer, 0, j)),
                  pl.BlockSpec((None, tf, d), lambda i, j: (layer, j, 0)),
                  _resident((1, d))],
        out_specs=pl.BlockSpec((tm, d), lambda i, j: (i, 0)),
        out_shape=jax.ShapeDtypeStruct((m, d), F32),
        scratch_shapes=[pltpu.VMEM((tm, d), BF16), pltpu.VMEM((tm, d), F32)],
        compiler_params=_params(("parallel", "arbitrary"), 52),
        name=name,
    )(x, g1.reshape(1, d), w1, w2, g2.reshape(1, d))


def _rglru_kernel(x_ref, y_ref, cw_ref, cb_ref, wr_ref, wi_ref, br_ref, bi_ref, lam_ref,
                  o_ref, xp_ref, a_ref, u_ref, *, chunk):
    s_len, c = x_ref.shape
    pad = SUBLANES
    xp_ref[0:pad, :] = jnp.zeros((pad, c), F32)
    xp_ref[pad:, :] = x_ref[...]
    softplus_neg = jax.nn.softplus(-lam_ref[...])
    cw = cw_ref[...]
    groups = chunk // SUBLANES
    row_in_group = lax.broadcasted_iota(jnp.int32, (groups, SUBLANES, c), 1)

    def sigmoid(v):
        return 0.5 * jnp.tanh(0.5 * v) + 0.5

    def chunk_body(ci, h):
        t0 = ci * chunk
        xc = cb_ref[...] + cw[CONV_WIDTH - 1:CONV_WIDTH] * xp_ref[t0 + pad:t0 + pad + chunk, :]
        for d in range(1, CONV_WIDTH):
            tap = cw[CONV_WIDTH - 1 - d:CONV_WIDTH - d]
            xc = xc + tap * xp_ref[t0 + pad - d:t0 + pad - d + chunk, :]
        xcb = xc.astype(BF16)
        r = sigmoid(jnp.dot(xcb, wr_ref[...], preferred_element_type=F32) + br_ref[...])
        ig = sigmoid(jnp.dot(xcb, wi_ref[...], preferred_element_type=F32) + bi_ref[...])
        a = jnp.exp(-LRU_C * r * softplus_neg)
        y = 1.0 - a * a
        u = jnp.where(y > 0.0, y * lax.rsqrt(y), 0.0) * (ig * xc)
        a = a.reshape(groups, SUBLANES, c)
        u = u.reshape(groups, SUBLANES, c)
        for d in (1, 2, 4):
            keep = row_in_group >= d
            a_prev = jnp.where(keep, pltpu.roll(a, d, 1), 1.0)
            u_prev = jnp.where(keep, pltpu.roll(u, d, 1), 0.0)
            u = a * u_prev + u
            a = a * a_prev
        a_ref[...] = a.reshape(chunk, c)
        u_ref[...] = u.reshape(chunk, c)

        def group_body(gi, h):
            g0 = pl.multiple_of(gi * SUBLANES, SUBLANES)
            hv = a_ref[pl.ds(g0, SUBLANES), :] * h + u_ref[pl.ds(g0, SUBLANES), :]
            u_ref[pl.ds(g0, SUBLANES), :] = hv
            return jnp.broadcast_to(hv[SUBLANES - 1:SUBLANES, :], (SUBLANES, c))

        h = lax.fori_loop(0, groups, group_body, h, unroll=8)
        o_ref[t0:t0 + chunk, :] = (
            u_ref[...] * y_ref[t0:t0 + chunk, :].astype(F32)).astype(o_ref.dtype)
        return h

    h = jnp.zeros((SUBLANES, c), F32)
    for ci in range(s_len // chunk):
        h = chunk_body(ci, h)


def _rglru(xb, yb, conv_w, conv_b, gate_w, gate_b, lru_lambda, *, batch, seq, chunk, name):
    width = xb.shape[-1]
    c = width // LRU_BLOCKS
    row = pl.BlockSpec((1, c), lambda b, n: (0, n))
    gate = lambda g: pl.BlockSpec((None, None, c, c), lambda b, n: (g, n, 0, 0))
    gate_bias = lambda g: pl.BlockSpec((None, 1, c), lambda b, n: (g, 0, n))
    return pl.pallas_call(
        functools.partial(_rglru_kernel, chunk=chunk),
        grid=(batch, LRU_BLOCKS),
        in_specs=[pl.BlockSpec((None, seq, c), lambda b, n: (b, 0, n)),
                  pl.BlockSpec((None, seq, c), lambda b, n: (b, 0, n)),
                  pl.BlockSpec((CONV_WIDTH, c), lambda b, n: (0, n)),
                  row, gate(0), gate(1), gate_bias(0), gate_bias(1), row],
        out_specs=pl.BlockSpec((None, seq, c), lambda b, n: (b, 0, n)),
        out_shape=jax.ShapeDtypeStruct((batch, seq, width), BF16),
        scratch_shapes=[pltpu.VMEM((seq + SUBLANES, c), F32),
                        pltpu.VMEM((chunk, c), F32),
                        pltpu.VMEM((chunk, c), F32)],
        compiler_params=_params(("parallel", "parallel"), 40),
        name=name,
    )(xb.reshape(batch, seq, width), yb.reshape(batch, seq, width), conv_w,
      conv_b.reshape(1, width), gate_w, gate_w, gate_b.reshape(2, 1, width),
      gate_b.reshape(2, 1, width), lru_lambda.reshape(1, width))


CAST_BLOCK_ELEMS = 2 * 1024 * 1024


def _cast_kernel(w_ref, o_ref):
    o_ref[...] = w_ref[...].astype(o_ref.dtype)


def _cast_bf16(w, *, name):
    layers, rows, cols = w.shape
    tr = min(rows, CAST_BLOCK_ELEMS // cols)
    spec = pl.BlockSpec((None, tr, cols), lambda l, i: (l, i, 0))
    return pl.pallas_call(
        _cast_kernel,
        grid=(layers, rows // tr),
        in_specs=[spec],
        out_specs=spec,
        out_shape=jax.ShapeDtypeStruct(w.shape, BF16),
        compiler_params=_params(("parallel", "parallel"), 32),
        name=name,
    )(w)


def _identity(v):
    return v


def _even_mixer(xf, g_pre, w_in, lq1, lk1, lq2, lk2, sub_g, sinks, w_out, g_post,
                *, batch, seq, lambda_init):
    m = batch * seq
    n_qk = 2 * DIFF_HEADS * HEAD_DIM
    n_v = DIFF_HEADS * 2 * HEAD_DIM
    n_sq = SWA_Q_HEADS * HEAD_DIM
    c0 = 2 * n_qk + n_v + n_sq
    ks = [w_in[:, c0 + h * HEAD_DIM:c0 + (h + 1) * HEAD_DIM] for h in range(SWA_KV_HEADS)]
    c1 = c0 + SWA_KV_HEADS * HEAD_DIM
    vs = [w_in[:, c1 + h * HEAD_DIM:c1 + (h + 1) * HEAD_DIM] for h in range(SWA_KV_HEADS)]
    w_main = jnp.concatenate(
        [w_in[:, :2 * n_qk], w_in[:, 2 * n_qk + n_v:c0]]
        + [t for k in ks for t in (k, k)] + [t for v in vs for t in (v, v)],
        axis=1).astype(BF16)
    w_v = w_in[:, 2 * n_qk:2 * n_qk + n_v].astype(BF16)

    tq = min(256, seq)
    tn = 512
    is_query = lambda c: c < n_qk or 2 * n_qk <= c < 2 * n_qk + n_sq
    main_scales = tuple(QK_SCALE_LOG2 if is_query(c) else 1.0
                        for c in range(0, w_main.shape[1], tn))
    proj, vt = _norm_proj(xf, g_pre, [w_main, w_v], [_identity, _identity], [BF16, BF16],
                          tm=min(512, m), tn=tn, col_scales=(main_scales, None),
                          transposed=(False, True), tk=tq, name="even_in_proj")
    proj3 = proj.reshape(batch, seq, proj.shape[1])
    a_out = _diff_attn(proj3, vt, lq1, lk1, lq2, lk2, sub_g, batch=batch, seq=seq, tq=tq,
                       lambda_init=lambda_init, name="diff_attn")
    swa_q_col = 2 * n_qk // (SWA_GROUP * HEAD_DIM)
    swa_k_col = (2 * n_qk + n_sq) // LANES
    b_out = _swa(proj3, sinks, batch=batch, seq=seq, q_col=swa_q_col, k_col=swa_k_col,
                 v_col=swa_k_col + SWA_KV_HEADS, name="swa")
    w_o = w_out.astype(BF16)
    return _out_proj([a_out.reshape(m, n_v), b_out.reshape(m, n_sq)],
                     [w_o[:n_v], w_o[n_v:]], xf, g_post, tm=min(512, m), sub=256,
                     name="even_out_proj")


def _odd_mixer(xf, g_pre, w_in, conv_w, conv_b, gate_w, gate_b, lru_lambda, w_out, g_post,
               *, batch, seq):
    m = batch * seq
    width = w_in.shape[1] // 2
    w_b = w_in.astype(BF16)
    yb, xb = _norm_proj(xf, g_pre, [w_b[:, :width], w_b[:, width:]],
                        [jax.nn.gelu, _identity], [BF16, F32],
                        tm=min(512, m), tn=512, name="odd_in_proj")
    hy = _rglru(xb, yb, conv_w, conv_b, gate_w.astype(BF16), gate_b, lru_lambda,
                batch=batch, seq=seq, chunk=min(512, seq), name="rglru")
    return _out_proj([hy.reshape(m, width)], [w_out.astype(BF16)], xf, g_post,
                     tm=min(512, m), sub=256, name="odd_out_proj")


def kernel(x, even_w_in, even_lam_q1, even_lam_k1, even_lam_q2, even_lam_k2, even_subln_g,
           even_sinks, even_w_out, odd_w_in, odd_conv_w, odd_conv_b, odd_gate_w, odd_gate_b,
           odd_lru_lambda, odd_w_out, pre_mix_g, post_mix_g, pre_mlp_g, post_mlp_g, mlp_w1,
           mlp_w2):
    batch, seq, d = x.shape
    m = batch * seq
    xf = x.reshape(m, d)
    depth = pre_mix_g.shape[0]
    w1_b = _cast_bf16(mlp_w1, name="cast_mlp_w1")
    w2_b = _cast_bf16(mlp_w2, name="cast_mlp_w2")
    for layer in range(depth):
        if layer % 2 == 0:
            e = layer // 2
            lambda_init = 0.8 - 0.6 * math.exp(-0.3 * layer)
            xf = _even_mixer(xf, pre_mix_g[layer], even_w_in[e], even_lam_q1[e],
                             even_lam_k1[e], even_lam_q2[e], even_lam_k2[e], even_subln_g[e],
                             even_sinks[e], even_w_out[e], post_mix_g[layer],
                             batch=batch, seq=seq, lambda_init=lambda_init)
        else:
            o = layer // 2
            xf = _odd_mixer(xf, pre_mix_g[layer], odd_w_in[o], odd_conv_w[o], odd_conv_b[o],
                            odd_gate_w[o], odd_gate_b[o], odd_lru_lambda[o], odd_w_out[o],
                            post_mix_g[layer], batch=batch, seq=seq)
        xf = _mlp(xf, pre_mlp_g[layer], w1_b, w2_b, post_mlp_g[layer], layer=layer,
                  tm=min(512, m), tf=1024, sub=min(256, m), name=f"mlp_{layer}")
    return xf.reshape(batch, seq, d)
```

```python
import functools
import math

import jax
import jax.numpy as jnp
from jax import lax
from jax.experimental import pallas as pl
from jax.experimental.pallas import tpu as pltpu

F32 = jnp.float32
BF16 = jnp.bfloat16

EPS = 1e-6
HEAD_DIM = 64
DIFF_HEADS = 8
SWA_Q_HEADS = 16
SWA_KV_HEADS = 2
SWA_GROUP = SWA_Q_HEADS // SWA_KV_HEADS
SWA_BLOCK = 128
LRU_BLOCKS = 8
CONV_WIDTH = 4
LRU_C = 8.0

LANES = 128
SUBLANES = 8
MIB = 1024 * 1024

NT_DIMS = (((1,), (1,)), ((), ()))


def _params(semantics, vmem_mib):
    return pltpu.CompilerParams(dimension_semantics=semantics,
                                vmem_limit_bytes=vmem_mib * MIB)


def _rmsnorm(x, g):
    ms = jnp.mean(x * x, axis=-1, keepdims=True)
    return x * lax.rsqrt(ms + EPS) * g


def _resident(shape):
    return pl.BlockSpec(shape, lambda *_: (0,) * len(shape), pipeline_mode=pl.Buffered(1))


def _norm_proj_kernel(x_ref, g_ref, *refs, epilogues, col_scales, transposed, tn, tk):
    n = len(epilogues)
    w_refs, o_refs = refs[:n], refs[n:2 * n]
    x = x_ref[...]
    rs = lax.rsqrt(jnp.mean(x * x, axis=-1, keepdims=True) + EPS)
    xg = (x * g_ref[...]).astype(BF16)
    jobs = sorted((c0, k) for k in range(n) for c0 in range(0, w_refs[k].shape[1], tn))
    for c0, k in jobs:
        w_ref, o_ref, scales = w_refs[k], o_refs[k], col_scales[k]
        row_scale = rs if scales is None else rs * scales[c0 // tn]
        res = epilogues[k](
            jnp.dot(xg, w_ref[:, c0:c0 + tn], preferred_element_type=F32) * row_scale)
        if transposed[k]:
            res_t = res.T.astype(o_ref.dtype)
            for c in range(o_ref.shape[0]):
                o_ref[c, c0:c0 + tn, :] = res_t[:, c * tk:(c + 1) * tk]
        else:
            o_ref[:, c0:c0 + tn] = res.astype(o_ref.dtype)


def _norm_proj(x, g, ws, epilogues, out_dtypes, *, tm, tn, name, col_scales=None,
               transposed=None, tk=None):
    m, d = x.shape
    transposed = transposed or (False,) * len(ws)
    col_scales = col_scales or (None,) * len(ws)
    out_specs, out_shape = [], []
    for w, dt, tr in zip(ws, out_dtypes, transposed):
        n = w.shape[1]
        if tr:
            out_specs.append(pl.BlockSpec((tm // tk, n, tk), lambda i: (i, 0, 0)))
            out_shape.append(jax.ShapeDtypeStruct((m // tk, n, tk), dt))
        else:
            out_specs.append(pl.BlockSpec((tm, n), lambda i: (i, 0)))
            out_shape.append(jax.ShapeDtypeStruct((m, n), dt))
    return pl.pallas_call(
        functools.partial(_norm_proj_kernel, epilogues=tuple(epilogues),
                          col_scales=tuple(col_scales), transposed=tuple(transposed),
                          tn=tn, tk=tk),
        grid=(m // tm,),
        in_specs=[pl.BlockSpec((tm, d), lambda i: (i, 0)), _resident((1, d))]
                 + [_resident(w.shape) for w in ws],
        out_specs=out_specs,
        out_shape=out_shape,
        compiler_params=_params(("parallel",), 52),
        name=name,
    )(x, g.reshape(1, d), *ws)


DIFF_HEADS_PER_STEP = 4
LOG2_E = 1.4426950408889634
QK_SCALE_LOG2 = HEAD_DIM ** -0.5 * LOG2_E
SUM_ROWS = 16


def _diff_attn_kernel(lq1_ref, lk1_ref, lq2_ref, lk2_ref, g_ref, q_ref, k_ref, vt_ref,
                      o_ref, *scratch, tq, lambda_init):
    hps = DIFF_HEADS_PER_STEP
    s_refs, qzt_refs, m_refs, acc_refs = (scratch[n * hps:(n + 1) * hps] for n in range(4))
    s_len = q_ref.shape[0]
    tk = tq
    heads = range(hps)
    ones_rows = jnp.ones((SUM_ROWS, tk), BF16)
    lam = (jnp.exp(jnp.sum(lq1_ref[...] * lk1_ref[...], keepdims=True))
           - jnp.exp(jnp.sum(lq2_ref[...] * lk2_ref[...], keepdims=True)) + lambda_init)
    first_map = lax.broadcasted_iota(jnp.int32, (LANES, tq), 0) < HEAD_DIM
    kpos = lax.broadcasted_iota(jnp.int32, (tk, 2 * tq), 0)
    qcol = lax.broadcasted_iota(jnp.int32, (tk, 2 * tq), 1)
    causal = kpos <= jnp.where(qcol >= tq, qcol - tq, qcol)

    def scores(hh, j):
        cols = slice(hh * LANES, (hh + 1) * LANES)
        k = k_ref[pl.ds(pl.multiple_of(j * tk, tk), tk), cols]
        s_refs[hh][...] = jnp.dot(k, qzt_refs[hh][...], preferred_element_type=F32)

    def consume(hh, j, masked):
        cols = slice(hh * LANES, (hh + 1) * LANES)
        st = s_refs[hh][...]
        if masked:
            st = jnp.where(causal, st, -jnp.inf)
        m_old = m_refs[hh][...]
        m_new = jnp.maximum(m_old, jnp.max(st, axis=0, keepdims=True))
        alpha = jnp.exp2(m_old - m_new)
        p = jnp.exp2(st - m_new).astype(BF16)
        vt_ones = jnp.concatenate([vt_ref[j, cols, :], ones_rows], axis=0)
        acc_refs[hh][...] = alpha * acc_refs[hh][...] + jnp.dot(
            vt_ones, p, preferred_element_type=F32)
        m_refs[hh][...] = m_new

    def start_q_tile(i):
        r0 = pl.multiple_of(i * tq, tq)
        for hh in heads:
            qt = q_ref[pl.ds(r0, tq), hh * LANES:(hh + 1) * LANES].astype(F32).T
            qzt_refs[hh][:, 0:tq] = jnp.where(first_map, qt, 0.0).astype(BF16)
            qzt_refs[hh][:, tq:2 * tq] = jnp.where(first_map, 0.0, qt).astype(BF16)
            scores(hh, 0)

    def reset_state():
        for hh in heads:
            m_refs[hh][...] = jnp.full(m_refs[hh].shape, -jnp.inf, F32)
            acc_refs[hh][...] = jnp.zeros(acc_refs[hh].shape, F32)

    def step(j):
        for hh in heads:
            consume(hh, j, False)
            scores(hh, j + 1)

    def finish_q_tile(i, odd):
        r0 = pl.multiple_of(i * tq, tq)
        if odd:
            step(i - 1)
        for hh in heads:
            consume(hh, i, True)
        start_q_tile(jnp.minimum(i + 1, n_q - 1))
        for hh in heads:
            acc = acc_refs[hh][0:LANES, :]
            inv_l = 1.0 / acc_refs[hh][LANES:LANES + 1, :]
            ot = acc[:, :tq] * inv_l[:, :tq] - lam * (acc[:, tq:] * inv_l[:, tq:])
            o = _rmsnorm(ot.T, g_ref[...]) * (1.0 - lambda_init)
            o_ref[pl.ds(r0, tq), hh * LANES:(hh + 1) * LANES] = o.astype(o_ref.dtype)
        reset_state()

    def q_body(i, carry):
        def pair(jp, c):
            step(2 * jp)
            step(2 * jp + 1)
            return c

        lax.fori_loop(0, i // 2, pair, 0)
        pl.when(i % 2 == 1)(lambda: finish_q_tile(i, True))
        pl.when(i % 2 == 0)(lambda: finish_q_tile(i, False))
        return carry

    n_q = s_len // tq
    reset_state()
    start_q_tile(0)
    lax.fori_loop(0, n_q, q_body, 0)


def _diff_attn(proj, vt, lq1, lk1, lq2, lk2, sub_g, *, batch, seq, tq, lambda_init, name):
    hps = DIFF_HEADS_PER_STEP
    width = hps * LANES
    groups = DIFF_HEADS // hps
    vec = pl.BlockSpec((1, HEAD_DIM), lambda b, h: (0, 0))
    return pl.pallas_call(
        functools.partial(_diff_attn_kernel, tq=tq, lambda_init=lambda_init),
        grid=(batch, groups),
        in_specs=[vec, vec, vec, vec,
                  pl.BlockSpec((1, LANES), lambda b, h: (0, 0)),
                  pl.BlockSpec((None, seq, width), lambda b, h: (b, 0, h)),
                  pl.BlockSpec((None, seq, width), lambda b, h: (b, 0, groups + h)),
                  pl.BlockSpec((seq // tq, width, tq), lambda b, h: (b, h, 0))],
        out_specs=pl.BlockSpec((None, seq, width), lambda b, h: (b, 0, h)),
        out_shape=jax.ShapeDtypeStruct((batch, seq, DIFF_HEADS * LANES), BF16),
        scratch_shapes=[pltpu.VMEM((tq, 2 * tq), F32)] * hps
                       + [pltpu.VMEM((LANES, 2 * tq), BF16)] * hps
                       + [pltpu.VMEM((1, 2 * tq), F32)] * hps
                       + [pltpu.VMEM((LANES + SUM_ROWS, 2 * tq), F32)] * hps,
        compiler_params=_params(("parallel", "parallel"), 48),
        name=name,
    )(lq1.reshape(1, -1), lk1.reshape(1, -1), lq2.reshape(1, -1), lk2.reshape(1, -1),
      sub_g.reshape(1, -1), proj, proj, vt)


SWA_BLOCKS_PER_TRIP = 4


def _swa_kernel(sinks_ref, q_ref, kk_ref, vv_ref, o_ref):
    s_len = q_ref.shape[0]
    blk = SWA_BLOCK
    kvh = pl.program_id(1)
    first_head = lax.broadcasted_iota(jnp.int32, (blk, LANES), 1) < HEAD_DIM
    sinks = [sinks_ref[kvh * SWA_GROUP + g] * LOG2_E for g in range(SWA_GROUP)]

    nkeys = 2 * blk
    col_minus_row = (lax.broadcasted_iota(jnp.int32, (blk, nkeys), 1)
                     - lax.broadcasted_iota(jnp.int32, (blk, nkeys), 0))

    def block_scores(n):
        r0 = pl.multiple_of(n * blk, blk)
        k0 = pl.multiple_of(jnp.maximum(n - 1, 0) * blk, blk)
        qparts = []
        for gp in range(SWA_GROUP // 2):
            qq = q_ref[pl.ds(r0, blk), gp * LANES:(gp + 1) * LANES].astype(F32)
            qparts.append(jnp.where(first_head, qq, 0.0).astype(BF16))
            qparts.append(jnp.where(first_head, 0.0, qq).astype(BF16))
        qs = jnp.concatenate(qparts, axis=0)
        kk = kk_ref[pl.ds(k0, nkeys), :]
        return lax.dot_general(qs, kk, NT_DIMS, preferred_element_type=F32)

    def block_finish(n, s):
        r0 = pl.multiple_of(n * blk, blk)
        k0 = pl.multiple_of(jnp.maximum(n - 1, 0) * blk, blk)
        back = (r0 - k0) - col_minus_row
        valid = jnp.logical_and(back >= 0, back < blk)
        ps = []
        for g in range(SWA_GROUP):
            sg = jnp.where(valid, s[g * blk:(g + 1) * blk], -jnp.inf)
            m = jnp.maximum(jnp.max(sg, axis=-1, keepdims=True), sinks[g])
            e = jnp.exp2(sg - m)
            denom = jnp.sum(e, axis=-1, keepdims=True) + jnp.exp2(sinks[g] - m)
            ps.append((e * (1.0 / denom)).astype(BF16))
        p = jnp.concatenate(ps, axis=0)
        o = jnp.dot(p, vv_ref[pl.ds(k0, nkeys), :], preferred_element_type=F32)
        for gp in range(SWA_GROUP // 2):
            lo = o[(2 * gp) * blk:(2 * gp + 1) * blk]
            hi = o[(2 * gp + 1) * blk:(2 * gp + 2) * blk]
            o_ref[pl.ds(r0, blk), gp * LANES:(gp + 1) * LANES] = jnp.where(
                first_head, lo, hi).astype(o_ref.dtype)

    def body(gi, c):
        blocks = [gi * SWA_BLOCKS_PER_TRIP + u for u in range(SWA_BLOCKS_PER_TRIP)]
        scores = [block_scores(n) for n in blocks]
        for n, s in zip(blocks, scores):
            block_finish(n, s)
        return c

    lax.fori_loop(0, s_len // (blk * SWA_BLOCKS_PER_TRIP), body, 0)


def _swa(proj, sinks, *, batch, seq, q_col, k_col, v_col, name):
    qw = SWA_GROUP * HEAD_DIM
    return pl.pallas_call(
        _swa_kernel,
        grid=(batch, SWA_KV_HEADS),
        in_specs=[pl.BlockSpec(memory_space=pltpu.SMEM),
                  pl.BlockSpec((None, seq, qw), lambda b, h: (b, 0, q_col + h)),
                  pl.BlockSpec((None, seq, LANES), lambda b, h: (b, 0, k_col + h)),
                  pl.BlockSpec((None, seq, LANES), lambda b, h: (b, 0, v_col + h))],
        out_specs=pl.BlockSpec((None, seq, qw), lambda b, h: (b, 0, h)),
        out_shape=jax.ShapeDtypeStruct((batch, seq, SWA_Q_HEADS * HEAD_DIM), BF16),
        compiler_params=_params(("parallel", "parallel"), 40),
        name=name,
    )(sinks, proj, proj, proj)


def _out_proj_kernel(*refs, n_in, sub):
    a_refs, w_refs = refs[:n_in], refs[n_in:2 * n_in]
    x_ref, g_ref, o_ref = refs[2 * n_in:]
    for r0 in range(0, x_ref.shape[0], sub):
        rows = slice(r0, r0 + sub)
        mix = jnp.dot(a_refs[0][rows, :], w_refs[0][...], preferred_element_type=F32)
        for a_ref, w_ref in zip(a_refs[1:], w_refs[1:]):
            mix = mix + jnp.dot(a_ref[rows, :], w_ref[...], preferred_element_type=F32)
        o_ref[rows, :] = x_ref[rows, :] + _rmsnorm(mix, g_ref[...])


def _out_proj(acts, ws, x, g, *, tm, sub, name):
    m, d = x.shape
    n_in = len(acts)
    return pl.pallas_call(
        functools.partial(_out_proj_kernel, n_in=n_in, sub=sub),
        grid=(m // tm,),
        in_specs=[pl.BlockSpec((tm, a.shape[1]), lambda i: (i, 0)) for a in acts]
                 + [_resident(w.shape) for w in ws]
                 + [pl.BlockSpec((tm, d), lambda i: (i, 0)), _resident((1, d))],
        out_specs=pl.BlockSpec((tm, d), lambda i: (i, 0)),
        out_shape=jax.ShapeDtypeStruct((m, d), F32),
        compiler_params=_params(("parallel",), 52),
        name=name,
    )(*acts, *ws, x, g.reshape(1, d))


def _mlp_kernel(x_ref, g1_ref, w1_ref, w2_ref, g2_ref, o_ref, hn_ref, acc_ref, *, sub):
    j = pl.program_id(1)
    last = pl.num_programs(1) - 1
    tm = x_ref.shape[0]

    def step(first, final, rows_per_pass):
        for r0 in range(0, tm, rows_per_pass):
            rows = slice(r0, r0 + rows_per_pass)
            if first:
                hn = _rmsnorm(x_ref[rows, :], g1_ref[...]).astype(BF16)
                hn_ref[rows, :] = hn
            else:
                hn = hn_ref[rows, :]
            z = jnp.maximum(jnp.dot(hn, w1_ref[...], preferred_element_type=F32), 0.0)
            part = jnp.dot((z * z).astype(BF16), w2_ref[...], preferred_element_type=F32)
            acc = part if first else acc_ref[rows, :] + part
            if final:
                o_ref[rows, :] = x_ref[rows, :] + _rmsnorm(acc, g2_ref[...])
            else:
                acc_ref[rows, :] = acc

    pl.when(j == 0)(lambda: step(True, False, sub))
    pl.when(jnp.logical_and(j > 0, j < last))(lambda: step(False, False, tm))
    pl.when(j == last)(lambda: step(False, True, sub))


def _mlp(x, g1, w1, w2, g2, *, layer, tm, tf, sub, name):
    m, d = x.shape
    f = w1.shape[2]
    assert f // tf >= 2, "the first and the last d_ff step must be different steps"
    return pl.pallas_call(
        functools.partial(_mlp_kernel, sub=sub),
        grid=(m // tm, f // tf),
        in_specs=[pl.BlockSpec((tm, d), lambda i, j: (i, 0)),
                  _resident((1, d)),
                  pl.BlockSpec((None, d, tf), lambda i, j: (layer, 0, j)),
                  pl.BlockSpec((None, tf, d), lambda i, j: (layer, j, 0)),
                  _resident((1, d))],
        out_specs=pl.BlockSpec((tm, d), lambda i, j: (i, 0)),
        out_shape=jax.ShapeDtypeStruct((m, d), F32),
        scratch_shapes=[pltpu.VMEM((tm, d), BF16), pltpu.VMEM((tm, d), F32)],
        compiler_params=_params(("parallel", "arbitrary"), 52),
        name=name,
    )(x, g1.reshape(1, d), w1, w2, g2.reshape(1, d))


def _rglru_kernel(x_ref, y_ref, cw_ref, cb_ref, wr_ref, wi_ref, br_ref, bi_ref, lam_ref,
                  o_ref, xp_ref, *, chunk):
    s_len, c = x_ref.shape
    pad = SUBLANES
    xp_ref[0:pad, :] = jnp.zeros((pad, c), F32)
    xp_ref[pad:, :] = x_ref[...]
    log2_a_slope = jax.nn.softplus(-lam_ref[...]) * (-0.5 * LRU_C * LOG2_E)
    cw = cw_ref[...]
    groups = chunk // SUBLANES
    row_in_group = lax.broadcasted_iota(jnp.int32, (groups, SUBLANES, c), 1)

    def chunk_body(ci, h):
        t0 = ci * chunk
        xc = cb_ref[...] + cw[CONV_WIDTH - 1:CONV_WIDTH] * xp_ref[t0 + pad:t0 + pad + chunk, :]
        for d in range(1, CONV_WIDTH):
            tap = cw[CONV_WIDTH - 1 - d:CONV_WIDTH - d]
            xc = xc + tap * xp_ref[t0 + pad - d:t0 + pad - d + chunk, :]
        xcb = xc.astype(BF16)
        tanh_r = jnp.tanh(jnp.dot(xcb, wr_ref[...], preferred_element_type=F32) + br_ref[...])
        tanh_i = jnp.tanh(jnp.dot(xcb, wi_ref[...], preferred_element_type=F32) + bi_ref[...])
        a = jnp.exp2(log2_a_slope * (tanh_r + 1.0))
        y = 1.0 - a * a
        gated_x = (0.5 * xc) * (tanh_i + 1.0)
        u = jnp.where(y > 0.0, y * lax.rsqrt(y), 0.0) * gated_x
        a = a.reshape(groups, SUBLANES, c)
        u = u.reshape(groups, SUBLANES, c)
        for d in (1, 2, 4):
            keep = row_in_group >= d
            a_prev = jnp.where(keep, pltpu.roll(a, d, 1), 1.0)
            u_prev = jnp.where(keep, pltpu.roll(u, d, 1), 0.0)
            u = a * u_prev + u
            a = a * a_prev
        states = []
        for g in range(groups):
            hv = a[g] * h + u[g]
            states.append(hv)
            h = jnp.broadcast_to(hv[SUBLANES - 1:SUBLANES, :], (SUBLANES, c))
        hs = jnp.concatenate(states, axis=0)
        o_ref[t0:t0 + chunk, :] = (
            hs * y_ref[t0:t0 + chunk, :].astype(F32)).astype(o_ref.dtype)
        return h

    h = jnp.zeros((SUBLANES, c), F32)
    for ci in range(s_len // chunk):
        h = chunk_body(ci, h)


def _rglru(xb, yb, conv_w, conv_b, gate_w, gate_b, lru_lambda, *, batch, seq, chunk, name):
    width = xb.shape[-1]
    c = width // LRU_BLOCKS
    row = pl.BlockSpec((1, c), lambda b, n: (0, n))
    gate = lambda g: pl.BlockSpec((None, None, c, c), lambda b, n: (g, n, 0, 0))
    gate_bias = lambda g: pl.BlockSpec((None, 1, c), lambda b, n: (g, 0, n))
    return pl.pallas_call(
        functools.partial(_rglru_kernel, chunk=chunk),
        grid=(batch, LRU_BLOCKS),
        in_specs=[pl.BlockSpec((None, seq, c), lambda b, n: (b, 0, n)),
                  pl.BlockSpec((None, seq, c), lambda b, n: (b, 0, n)),
                  pl.BlockSpec((CONV_WIDTH, c), lambda b, n: (0, n)),
                  row, gate(0), gate(1), gate_bias(0), gate_bias(1), row],
        out_specs=pl.BlockSpec((None, seq, c), lambda b, n: (b, 0, n)),
        out_shape=jax.ShapeDtypeStruct((batch, seq, width), BF16),
        scratch_shapes=[pltpu.VMEM((seq + SUBLANES, c), F32)],
        compiler_params=_params(("parallel", "parallel"), 40),
        name=name,
    )(xb.reshape(batch, seq, width), yb.reshape(batch, seq, width), conv_w,
      conv_b.reshape(1, width), gate_w, gate_w, gate_b.reshape(2, 1, width),
      gate_b.reshape(2, 1, width), lru_lambda.reshape(1, width))


CAST_BLOCK_ELEMS = 2 * 1024 * 1024


def _cast_kernel(w_ref, o_ref):
    o_ref[...] = w_ref[...].astype(o_ref.dtype)


def _cast_bf16(w, *, name):
    layers, rows, cols = w.shape
    tr = min(rows, CAST_BLOCK_ELEMS // cols)
    spec = pl.BlockSpec((None, tr, cols), lambda l, i: (l, i, 0))
    return pl.pallas_call(
        _cast_kernel,
        grid=(layers, rows // tr),
        in_specs=[spec],
        out_specs=spec,
        out_shape=jax.ShapeDtypeStruct(w.shape, BF16),
        compiler_params=_params(("parallel", "parallel"), 32),
        name=name,
    )(w)


def _identity(v):
    return v


def _even_mixer(xf, g_pre, w_in, lq1, lk1, lq2, lk2, sub_g, sinks, w_out, g_post,
                *, batch, seq, lambda_init):
    m = batch * seq
    n_qk = 2 * DIFF_HEADS * HEAD_DIM
    n_v = DIFF_HEADS * 2 * HEAD_DIM
    n_sq = SWA_Q_HEADS * HEAD_DIM
    c0 = 2 * n_qk + n_v + n_sq
    ks = [w_in[:, c0 + h * HEAD_DIM:c0 + (h + 1) * HEAD_DIM] for h in range(SWA_KV_HEADS)]
    c1 = c0 + SWA_KV_HEADS * HEAD_DIM
    vs = [w_in[:, c1 + h * HEAD_DIM:c1 + (h + 1) * HEAD_DIM] for h in range(SWA_KV_HEADS)]
    w_main = jnp.concatenate(
        [w_in[:, :2 * n_qk], w_in[:, 2 * n_qk + n_v:c0]]
        + [t for k in ks for t in (k, k)] + [t for v in vs for t in (v, v)],
        axis=1).astype(BF16)
    w_v = w_in[:, 2 * n_qk:2 * n_qk + n_v].astype(BF16)

    tq = min(256, seq)
    tn = 512
    is_query = lambda c: c < n_qk or 2 * n_qk <= c < 2 * n_qk + n_sq
    main_scales = tuple(QK_SCALE_LOG2 if is_query(c) else 1.0
                        for c in range(0, w_main.shape[1], tn))
    proj, vt = _norm_proj(xf, g_pre, [w_main, w_v], [_identity, _identity], [BF16, BF16],
                          tm=min(512, m), tn=tn, col_scales=(main_scales, None),
                          transposed=(False, True), tk=tq, name="even_in_proj")
    proj3 = proj.reshape(batch, seq, proj.shape[1])
    a_out = _diff_attn(proj3, vt, lq1, lk1, lq2, lk2, sub_g, batch=batch, seq=seq, tq=tq,
                       lambda_init=lambda_init, name="diff_attn")
    swa_q_col = 2 * n_qk // (SWA_GROUP * HEAD_DIM)
    swa_k_col = (2 * n_qk + n_sq) // LANES
    b_out = _swa(proj3, sinks, batch=batch, seq=seq, q_col=swa_q_col, k_col=swa_k_col,
                 v_col=swa_k_col + SWA_KV_HEADS, name="swa")
    w_o = w_out.astype(BF16)
    return _out_proj([a_out.reshape(m, n_v), b_out.reshape(m, n_sq)],
                     [w_o[:n_v], w_o[n_v:]], xf, g_post, tm=min(512, m), sub=256,
                     name="even_out_proj")


def _odd_mixer(xf, g_pre, w_in, conv_w, conv_b, gate_w, gate_b, lru_lambda, w_out, g_post,
               *, batch, seq):
    m = batch * seq
    width = w_in.shape[1] // 2
    w_b = w_in.astype(BF16)
    yb, xb = _norm_proj(xf, g_pre, [w_b[:, :width], w_b[:, width:]],
                        [jax.nn.gelu, _identity], [BF16, F32],
                        tm=min(512, m), tn=512, name="odd_in_proj")
    hy = _rglru(xb, yb, conv_w, conv_b, (0.5 * gate_w).astype(BF16), 0.5 * gate_b, lru_lambda,
                batch=batch, seq=seq, chunk=min(512, seq), name="rglru")
    return _out_proj([hy.reshape(m, width)], [w_out.astype(BF16)], xf, g_post,
                     tm=min(512, m), sub=256, name="odd_out_proj")


def kernel(x, even_w_in, even_lam_q1, even_lam_k1, even_lam_q2, even_lam_k2, even_subln_g,
           even_sinks, even_w_out, odd_w_in, odd_conv_w, odd_conv_b, odd_gate_w, odd_gate_b,
           odd_lru_lambda, odd_w_out, pre_mix_g, post_mix_g, pre_mlp_g, post_mlp_g, mlp_w1,
           mlp_w2):
    batch, seq, d = x.shape
    m = batch * seq
    xf = x.reshape(m, d)
    depth = pre_mix_g.shape[0]
    w1_b = _cast_bf16(mlp_w1, name="cast_mlp_w1")
    w2_b = _cast_bf16(mlp_w2, name="cast_mlp_w2")
    for layer in range(depth):
        if layer % 2 == 0:
            e = layer // 2
            lambda_init = 0.8 - 0.6 * math.exp(-0.3 * layer)
            xf = _even_mixer(xf, pre_mix_g[layer], even_w_in[e], even_lam_q1[e],
                             even_lam_k1[e], even_lam_q2[e], even_lam_k2[e], even_subln_g[e],
                             even_sinks[e], even_w_out[e], post_mix_g[layer],
                             batch=batch, seq=seq, lambda_init=lambda_init)
        else:
            o = layer // 2
            xf = _odd_mixer(xf, pre_mix_g[layer], odd_w_in[o], odd_conv_w[o], odd_conv_b[o],
                            odd_gate_w[o], odd_gate_b[o], odd_lru_lambda[o], odd_w_out[o],
                            post_mix_g[layer], batch=batch, seq=seq)
        xf = _mlp(xf, pre_mlp_g[layer], w1_b, w2_b, post_mlp_g[layer], layer=layer,
                  tm=min(512, m), tf=1024, sub=min(256, m), name=f"mlp_{layer}")
    return xf.reshape(batch, seq, d)
```

```python
import functools
import math

import jax
import jax.numpy as jnp
from jax import lax
from jax.experimental import pallas as pl
from jax.experimental.pallas import tpu as pltpu

F32 = jnp.float32
BF16 = jnp.bfloat16

EPS = 1e-6
HEAD_DIM = 64
DIFF_HEADS = 8
SWA_Q_HEADS = 16
SWA_KV_HEADS = 2
SWA_GROUP = SWA_Q_HEADS // SWA_KV_HEADS
SWA_BLOCK = 128
LRU_BLOCKS = 8
CONV_WIDTH = 4
LRU_C = 8.0

LANES = 128
SUBLANES = 8
MIB = 1024 * 1024

NT_DIMS = (((1,), (1,)), ((), ()))


def _params(semantics, vmem_mib):
    return pltpu.CompilerParams(dimension_semantics=semantics,
                                vmem_limit_bytes=vmem_mib * MIB)


def _rmsnorm(x, g):
    ms = jnp.mean(x * x, axis=-1, keepdims=True)
    return x * lax.rsqrt(ms + EPS) * g


def _resident(shape):
    return pl.BlockSpec(shape, lambda *_: (0,) * len(shape), pipeline_mode=pl.Buffered(1))


def _norm_proj_kernel(x_ref, g_ref, *refs, epilogues, col_scales, transposed, tn, tk):
    n = len(epilogues)
    w_refs, o_refs = refs[:n], refs[n:2 * n]
    x = x_ref[...]
    rs = lax.rsqrt(jnp.mean(x * x, axis=-1, keepdims=True) + EPS)
    xg = (x * g_ref[...]).astype(BF16)
    jobs = sorted((c0, k) for k in range(n) for c0 in range(0, w_refs[k].shape[1], tn))
    for c0, k in jobs:
        w_ref, o_ref, scales = w_refs[k], o_refs[k], col_scales[k]
        row_scale = rs if scales is None else rs * scales[c0 // tn]
        res = epilogues[k](
            jnp.dot(xg, w_ref[:, c0:c0 + tn], preferred_element_type=F32) * row_scale)
        if transposed[k]:
            res_t = res.T.astype(o_ref.dtype)
            for c in range(o_ref.shape[0]):
                o_ref[c, c0:c0 + tn, :] = res_t[:, c * tk:(c + 1) * tk]
        else:
            o_ref[:, c0:c0 + tn] = res.astype(o_ref.dtype)


def _norm_proj(x, g, ws, epilogues, out_dtypes, *, tm, tn, name, col_scales=None,
               transposed=None, tk=None):
    m, d = x.shape
    transposed = transposed or (False,) * len(ws)
    col_scales = col_scales or (None,) * len(ws)
    out_specs, out_shape = [], []
    for w, dt, tr in zip(ws, out_dtypes, transposed):
        n = w.shape[1]
        if tr:
            out_specs.append(pl.BlockSpec((tm // tk, n, tk), lambda i: (i, 0, 0)))
            out_shape.append(jax.ShapeDtypeStruct((m // tk, n, tk), dt))
        else:
            out_specs.append(pl.BlockSpec((tm, n), lambda i: (i, 0)))
            out_shape.append(jax.ShapeDtypeStruct((m, n), dt))
    return pl.pallas_call(
        functools.partial(_norm_proj_kernel, epilogues=tuple(epilogues),
                          col_scales=tuple(col_scales), transposed=tuple(transposed),
                          tn=tn, tk=tk),
        grid=(m // tm,),
        in_specs=[pl.BlockSpec((tm, d), lambda i: (i, 0)), _resident((1, d))]
                 + [_resident(w.shape) for w in ws],
        out_specs=out_specs,
        out_shape=out_shape,
        compiler_params=_params(("parallel",), 52),
        name=name,
    )(x, g.reshape(1, d), *ws)


DIFF_HEADS_PER_STEP = 4
LOG2_E = 1.4426950408889634
QK_SCALE_LOG2 = HEAD_DIM ** -0.5 * LOG2_E
SUM_ROWS = 16


def _diff_attn_kernel(lq1_ref, lk1_ref, lq2_ref, lk2_ref, g_ref, q_ref, k_ref, vt_ref,
                      o_ref, *scratch, tq, lambda_init):
    hps = DIFF_HEADS_PER_STEP
    s_refs, qzt_refs, m_refs, acc_refs = (scratch[n * hps:(n + 1) * hps] for n in range(4))
    s_len = q_ref.shape[0]
    tk = tq
    heads = range(hps)
    last_head = hps - 1
    ones_rows = jnp.ones((SUM_ROWS, tk), BF16)
    lam = (jnp.exp(jnp.sum(lq1_ref[...] * lk1_ref[...], keepdims=True))
           - jnp.exp(jnp.sum(lq2_ref[...] * lk2_ref[...], keepdims=True)) + lambda_init)
    first_map = lax.broadcasted_iota(jnp.int32, (LANES, tq), 0) < HEAD_DIM
    kpos = lax.broadcasted_iota(jnp.int32, (tk, 2 * tq), 0)
    qcol = lax.broadcasted_iota(jnp.int32, (tk, 2 * tq), 1)
    causal = kpos <= jnp.where(qcol >= tq, qcol - tq, qcol)

    def scores(hh, j):
        cols = slice(hh * LANES, (hh + 1) * LANES)
        k = k_ref[pl.ds(pl.multiple_of(j * tk, tk), tk), cols]
        s_refs[hh][...] = jnp.dot(k, qzt_refs[hh][...], preferred_element_type=F32)

    def consume(hh, j, masked):
        cols = slice(hh * LANES, (hh + 1) * LANES)
        st = s_refs[hh][...]
        if masked:
            st = jnp.where(causal, st, -jnp.inf)
        m_old = m_refs[hh][...]
        m_new = jnp.maximum(m_old, jnp.max(st, axis=0, keepdims=True))
        alpha = jnp.exp2(m_old - m_new)
        p = jnp.exp2(st - m_new).astype(BF16)
        vt_ones = jnp.concatenate([vt_ref[j, cols, :], ones_rows], axis=0)
        acc_refs[hh][...] = alpha * acc_refs[hh][...] + jnp.dot(
            vt_ones, p, preferred_element_type=F32)
        m_refs[hh][...] = m_new

    def start_q_tile(i):
        r0 = pl.multiple_of(i * tq, tq)
        for hh in heads:
            qt = q_ref[pl.ds(r0, tq), hh * LANES:(hh + 1) * LANES].astype(F32).T
            qzt_refs[hh][:, 0:tq] = jnp.where(first_map, qt, 0.0).astype(BF16)
            qzt_refs[hh][:, tq:2 * tq] = jnp.where(first_map, 0.0, qt).astype(BF16)
            if hh != last_head:
                scores(hh, 0)

    def reset_state():
        for hh in heads:
            m_refs[hh][...] = jnp.full(m_refs[hh].shape, -jnp.inf, F32)
            acc_refs[hh][...] = jnp.zeros(acc_refs[hh].shape, F32)

    def step(j):
        scores(last_head, j)
        for hh in heads:
            consume(hh, j, False)
            if hh != last_head:
                scores(hh, j + 1)

    def finish_q_tile(i, odd):
        r0 = pl.multiple_of(i * tq, tq)
        if odd:
            step(i - 1)
        scores(last_head, i)
        for hh in heads:
            consume(hh, i, True)
        start_q_tile(jnp.minimum(i + 1, n_q - 1))
        for hh in heads:
            acc = acc_refs[hh][0:LANES, :]
            inv_l = 1.0 / acc_refs[hh][LANES:LANES + 1, :]
            ot = acc[:, :tq] * inv_l[:, :tq] - lam * (acc[:, tq:] * inv_l[:, tq:])
            o = _rmsnorm(ot.T, g_ref[...]) * (1.0 - lambda_init)
            o_ref[pl.ds(r0, tq), hh * LANES:(hh + 1) * LANES] = o.astype(o_ref.dtype)
        reset_state()

    def q_body(i, carry):
        def pair(jp, c):
            step(2 * jp)
            step(2 * jp + 1)
            return c

        lax.fori_loop(0, i // 2, pair, 0)
        pl.when(i % 2 == 1)(lambda: finish_q_tile(i, True))
        pl.when(i % 2 == 0)(lambda: finish_q_tile(i, False))
        return carry

    n_q = s_len // tq
    reset_state()
    start_q_tile(0)
    lax.fori_loop(0, n_q, q_body, 0)


def _diff_attn(proj, vt, lq1, lk1, lq2, lk2, sub_g, *, batch, seq, tq, lambda_init, name):
    hps = DIFF_HEADS_PER_STEP
    width = hps * LANES
    groups = DIFF_HEADS // hps
    vec = pl.BlockSpec((1, HEAD_DIM), lambda b, h: (0, 0))
    return pl.pallas_call(
        functools.partial(_diff_attn_kernel, tq=tq, lambda_init=lambda_init),
        grid=(batch, groups),
        in_specs=[vec, vec, vec, vec,
                  pl.BlockSpec((1, LANES), lambda b, h: (0, 0)),
                  pl.BlockSpec((None, seq, width), lambda b, h: (b, 0, h)),
                  pl.BlockSpec((None, seq, width), lambda b, h: (b, 0, groups + h)),
                  pl.BlockSpec((seq // tq, width, tq), lambda b, h: (b, h, 0))],
        out_specs=pl.BlockSpec((None, seq, width), lambda b, h: (b, 0, h)),
        out_shape=jax.ShapeDtypeStruct((batch, seq, DIFF_HEADS * LANES), BF16),
        scratch_shapes=[pltpu.VMEM((tq, 2 * tq), F32)] * hps
                       + [pltpu.VMEM((LANES, 2 * tq), BF16)] * hps
                       + [pltpu.VMEM((1, 2 * tq), F32)] * hps
                       + [pltpu.VMEM((LANES + SUM_ROWS, 2 * tq), F32)] * hps,
        compiler_params=_params(("parallel", "parallel"), 48),
        name=name,
    )(lq1.reshape(1, -1), lk1.reshape(1, -1), lq2.reshape(1, -1), lk2.reshape(1, -1),
      sub_g.reshape(1, -1), proj, proj, vt)


SWA_BLOCKS_PER_TRIP = 4


def _swa_kernel(sinks_ref, q_ref, kk_ref, vv_ref, o_ref):
    s_len = q_ref.shape[0]
    blk = SWA_BLOCK
    kvh = pl.program_id(1)
    first_head = lax.broadcasted_iota(jnp.int32, (blk, LANES), 1) < HEAD_DIM
    sinks = [sinks_ref[kvh * SWA_GROUP + g] * LOG2_E for g in range(SWA_GROUP)]

    nkeys = 2 * blk
    col_minus_row = (lax.broadcasted_iota(jnp.int32, (blk, nkeys), 1)
                     - lax.broadcasted_iota(jnp.int32, (blk, nkeys), 0))

    def block_scores(n):
        r0 = pl.multiple_of(n * blk, blk)
        k0 = pl.multiple_of(jnp.maximum(n - 1, 0) * blk, blk)
        qparts = []
        for gp in range(SWA_GROUP // 2):
            qq = q_ref[pl.ds(r0, blk), gp * LANES:(gp + 1) * LANES].astype(F32)
            qparts.append(jnp.where(first_head, qq, 0.0).astype(BF16))
            qparts.append(jnp.where(first_head, 0.0, qq).astype(BF16))
        qs = jnp.concatenate(qparts, axis=0)
        kk = kk_ref[pl.ds(k0, nkeys), :]
        return lax.dot_general(qs, kk, NT_DIMS, preferred_element_type=F32)

    def block_finish(n, s):
        r0 = pl.multiple_of(n * blk, blk)
        k0 = pl.multiple_of(jnp.maximum(n - 1, 0) * blk, blk)
        back = (r0 - k0) - col_minus_row
        valid = jnp.logical_and(back >= 0, back < blk)
        ps = []
        for g in range(SWA_GROUP):
            sg = jnp.where(valid, s[g * blk:(g + 1) * blk], -jnp.inf)
            m = jnp.maximum(jnp.max(sg, axis=-1, keepdims=True), sinks[g])
            e = jnp.exp2(sg - m)
            denom = jnp.sum(e, axis=-1, keepdims=True) + jnp.exp2(sinks[g] - m)
            ps.append((e * (1.0 / denom)).astype(BF16))
        p = jnp.concatenate(ps, axis=0)
        o = jnp.dot(p, vv_ref[pl.ds(k0, nkeys), :], preferred_element_type=F32)
        for gp in range(SWA_GROUP // 2):
            lo = o[(2 * gp) * blk:(2 * gp + 1) * blk]
            hi = o[(2 * gp + 1) * blk:(2 * gp + 2) * blk]
            o_ref[pl.ds(r0, blk), gp * LANES:(gp + 1) * LANES] = jnp.where(
                first_head, lo, hi).astype(o_ref.dtype)

    def body(gi, c):
        blocks = [gi * SWA_BLOCKS_PER_TRIP + u for u in range(SWA_BLOCKS_PER_TRIP)]
        scores = [block_scores(n) for n in blocks]
        for n, s in zip(blocks, scores):
            block_finish(n, s)
        return c

    lax.fori_loop(0, s_len // (blk * SWA_BLOCKS_PER_TRIP), body, 0)


def _swa(q, kv, sinks, *, batch, seq, name):
    qw = SWA_GROUP * HEAD_DIM
    return pl.pallas_call(
        _swa_kernel,
        grid=(batch, SWA_KV_HEADS),
        in_specs=[pl.BlockSpec(memory_space=pltpu.SMEM),
                  pl.BlockSpec((None, seq, qw), lambda b, h: (b, 0, h)),
                  pl.BlockSpec((None, seq, LANES), lambda b, h: (b, 0, h)),
                  pl.BlockSpec((None, seq, LANES), lambda b, h: (b, 0, SWA_KV_HEADS + h))],
        out_specs=pl.BlockSpec((None, seq, qw), lambda b, h: (b, 0, h)),
        out_shape=jax.ShapeDtypeStruct((batch, seq, SWA_Q_HEADS * HEAD_DIM), BF16),
        compiler_params=_params(("parallel", "parallel"), 40),
        name=name,
    )(sinks, q, kv, kv)


def _out_proj_kernel(*refs, n_in, sub):
    a_refs, w_refs = refs[:n_in], refs[n_in:2 * n_in]
    x_ref, g_ref, o_ref = refs[2 * n_in:]
    for r0 in range(0, x_ref.shape[0], sub):
        rows = slice(r0, r0 + sub)
        mix = jnp.dot(a_refs[0][rows, :], w_refs[0][...], preferred_element_type=F32)
        for a_ref, w_ref in zip(a_refs[1:], w_refs[1:]):
            mix = mix + jnp.dot(a_ref[rows, :], w_ref[...], preferred_element_type=F32)
        o_ref[rows, :] = x_ref[rows, :] + _rmsnorm(mix, g_ref[...])


def _out_proj(acts, ws, x, g, *, tm, sub, name):
    m, d = x.shape
    n_in = len(acts)
    return pl.pallas_call(
        functools.partial(_out_proj_kernel, n_in=n_in, sub=sub),
        grid=(m // tm,),
        in_specs=[pl.BlockSpec((tm, a.shape[1]), lambda i: (i, 0)) for a in acts]
                 + [_resident(w.shape) for w in ws]
                 + [pl.BlockSpec((tm, d), lambda i: (i, 0)), _resident((1, d))],
        out_specs=pl.BlockSpec((tm, d), lambda i: (i, 0)),
        out_shape=jax.ShapeDtypeStruct((m, d), F32),
        compiler_params=_params(("parallel",), 52),
        name=name,
    )(*acts, *ws, x, g.reshape(1, d))


def _mlp_kernel(x_ref, g1_ref, w1_ref, w2_ref, g2_ref, o_ref, hn_ref, acc_ref, *, sub):
    j = pl.program_id(1)
    last = pl.num_programs(1) - 1
    tm = x_ref.shape[0]

    def step(first, final, rows_per_pass):
        for r0 in range(0, tm, rows_per_pass):
            rows = slice(r0, r0 + rows_per_pass)
            if first:
                hn = _rmsnorm(x_ref[rows, :], g1_ref[...]).astype(BF16)
                hn_ref[rows, :] = hn
            else:
                hn = hn_ref[rows, :]
            z = jnp.maximum(jnp.dot(hn, w1_ref[...], preferred_element_type=F32), 0.0)
            part = jnp.dot((z * z).astype(BF16), w2_ref[...], preferred_element_type=F32)
            acc = part if first else acc_ref[rows, :] + part
            if final:
                o_ref[rows, :] = x_ref[rows, :] + _rmsnorm(acc, g2_ref[...])
            else:
                acc_ref[rows, :] = acc

    pl.when(j == 0)(lambda: step(True, False, sub))
    pl.when(jnp.logical_and(j > 0, j < last))(lambda: step(False, False, tm))
    pl.when(j == last)(lambda: step(False, True, sub))


def _mlp(x, g1, w1, w2, g2, *, layer, tm, tf, sub, name):
    m, d = x.shape
    f = w1.shape[2]
    assert f // tf >= 2, "the first and the last d_ff step must be different steps"
    return pl.pallas_call(
        functools.partial(_mlp_kernel, sub=sub),
        grid=(m // tm, f // tf),
        in_specs=[pl.BlockSpec((tm, d), lambda i, j: (i, 0)),
                  _resident((1, d)),
                  pl.BlockSpec((None, d, tf), lambda i, j: (layer, 0, j)),
                  pl.BlockSpec((None, tf, d), lambda i, j: (layer, j, 0)),
                  _resident((1, d))],
        out_specs=pl.BlockSpec((tm, d), lambda i, j: (i, 0)),
        out_shape=jax.ShapeDtypeStruct((m, d), F32),
        scratch_shapes=[pltpu.VMEM((tm, d), BF16), pltpu.VMEM((tm, d), F32)],
        compiler_params=_params(("parallel", "arbitrary"), 52),
        name=name,
    )(x, g1.reshape(1, d), w1, w2, g2.reshape(1, d))


def _rglru_kernel(x_ref, y_ref, cw_ref, cb_ref, wr_ref, wi_ref, br_ref, bi_ref, lam_ref,
                  o_ref, xp_ref, *, chunk):
    s_len, c = x_ref.shape
    pad = SUBLANES
    xp_ref[0:pad, :] = jnp.zeros((pad, c), F32)
    xp_ref[pad:, :] = x_ref[...]
    log2_a_slope = jax.nn.softplus(-lam_ref[...]) * (-0.5 * LRU_C * LOG2_E)
    cw = cw_ref[...]
    groups = chunk // SUBLANES
    row_in_group = lax.broadcasted_iota(jnp.int32, (groups, SUBLANES, c), 1)

    def chunk_body(ci, h):
        t0 = ci * chunk
        xc = cb_ref[...] + cw[CONV_WIDTH - 1:CONV_WIDTH] * xp_ref[t0 + pad:t0 + pad + chunk, :]
        for d in range(1, CONV_WIDTH):
            tap = cw[CONV_WIDTH - 1 - d:CONV_WIDTH - d]
            xc = xc + tap * xp_ref[t0 + pad - d:t0 + pad - d + chunk, :]
        xcb = xc.astype(BF16)
        tanh_r = jnp.tanh(jnp.dot(xcb, wr_ref[...], preferred_element_type=F32) + br_ref[...])
        tanh_i = jnp.tanh(jnp.dot(xcb, wi_ref[...], preferred_element_type=F32) + bi_ref[...])
        a = jnp.exp2(log2_a_slope * (tanh_r + 1.0))
        y = 1.0 - a * a
        gated_x = (0.5 * xc) * (tanh_i + 1.0)
        u = jnp.where(y > 0.0, y * lax.rsqrt(y), 0.0) * gated_x
        a = a.reshape(groups, SUBLANES, c)
        u = u.reshape(groups, SUBLANES, c)
        for d in (1, 2, 4):
            keep = row_in_group >= d
            a_prev = jnp.where(keep, pltpu.roll(a, d, 1), 1.0)
            u_prev = jnp.where(keep, pltpu.roll(u, d, 1), 0.0)
            u = a * u_prev + u
            a = a * a_prev
        states = []
        for g in range(groups):
            hv = a[g] * h + u[g]
            states.append(hv)
            h = jnp.broadcast_to(hv[SUBLANES - 1:SUBLANES, :], (SUBLANES, c))
        hs = jnp.concatenate(states, axis=0)
        o_ref[t0:t0 + chunk, :] = (
            hs * y_ref[t0:t0 + chunk, :].astype(F32)).astype(o_ref.dtype)
        return h

    h = jnp.zeros((SUBLANES, c), F32)
    for ci in range(s_len // chunk):
        h = chunk_body(ci, h)


def _rglru(xb, yb, conv_w, conv_b, gate_w, gate_b, lru_lambda, *, batch, seq, chunk, name):
    width = xb.shape[-1]
    c = width // LRU_BLOCKS
    row = pl.BlockSpec((1, c), lambda b, n: (0, n))
    gate = lambda g: pl.BlockSpec((None, None, c, c), lambda b, n: (g, n, 0, 0))
    gate_bias = lambda g: pl.BlockSpec((None, 1, c), lambda b, n: (g, 0, n))
    return pl.pallas_call(
        functools.partial(_rglru_kernel, chunk=chunk),
        grid=(batch, LRU_BLOCKS),
        in_specs=[pl.BlockSpec((None, seq, c), lambda b, n: (b, 0, n)),
                  pl.BlockSpec((None, seq, c), lambda b, n: (b, 0, n)),
                  pl.BlockSpec((CONV_WIDTH, c), lambda b, n: (0, n)),
                  row, gate(0), gate(1), gate_bias(0), gate_bias(1), row],
        out_specs=pl.BlockSpec((None, seq, c), lambda b, n: (b, 0, n)),
        out_shape=jax.ShapeDtypeStruct((batch, seq, width), BF16),
        scratch_shapes=[pltpu.VMEM((seq + SUBLANES, c), F32)],
        compiler_params=_params(("parallel", "parallel"), 40),
        name=name,
    )(xb.reshape(batch, seq, width), yb.reshape(batch, seq, width), conv_w,
      conv_b.reshape(1, width), gate_w, gate_w, gate_b.reshape(2, 1, width),
      gate_b.reshape(2, 1, width), lru_lambda.reshape(1, width))


CAST_BLOCK_ELEMS = 2 * 1024 * 1024


def _cast_kernel(w_ref, o_ref):
    o_ref[...] = w_ref[...].astype(o_ref.dtype)


def _cast_bf16(w, *, name):
    layers, rows, cols = w.shape
    tr = min(rows, CAST_BLOCK_ELEMS // cols)
    spec = pl.BlockSpec((None, tr, cols), lambda l, i: (l, i, 0))
    return pl.pallas_call(
        _cast_kernel,
        grid=(layers, rows // tr),
        in_specs=[spec],
        out_specs=spec,
        out_shape=jax.ShapeDtypeStruct(w.shape, BF16),
        compiler_params=_params(("parallel", "parallel"), 32),
        name=name,
    )(w)


def _identity(v):
    return v


def _even_mixer(xf, g_pre, w_in, lq1, lk1, lq2, lk2, sub_g, sinks, w_out, g_post,
                *, batch, seq, lambda_init):
    m = batch * seq
    n_qk = 2 * DIFF_HEADS * HEAD_DIM
    n_v = DIFF_HEADS * 2 * HEAD_DIM
    n_sq = SWA_Q_HEADS * HEAD_DIM
    c0 = 2 * n_qk + n_v + n_sq
    ks = [w_in[:, c0 + h * HEAD_DIM:c0 + (h + 1) * HEAD_DIM] for h in range(SWA_KV_HEADS)]
    c1 = c0 + SWA_KV_HEADS * HEAD_DIM
    vs = [w_in[:, c1 + h * HEAD_DIM:c1 + (h + 1) * HEAD_DIM] for h in range(SWA_KV_HEADS)]
    w_qk = w_in[:, :2 * n_qk].astype(BF16)
    w_v = w_in[:, 2 * n_qk:2 * n_qk + n_v].astype(BF16)
    w_sq = w_in[:, 2 * n_qk + n_v:c0].astype(BF16)
    w_skv = jnp.concatenate([t for k in ks for t in (k, k)] + [t for v in vs for t in (v, v)],
                            axis=1).astype(BF16)

    tq = min(256, seq)
    tn = 512
    qk_scales = tuple(QK_SCALE_LOG2 if c < n_qk else 1.0 for c in range(0, 2 * n_qk, tn))
    sq_scales = (QK_SCALE_LOG2,) * (n_sq // tn)
    qk, vt, sq, skv = _norm_proj(
        xf, g_pre, [w_qk, w_v, w_sq, w_skv], [_identity] * 4, [BF16] * 4,
        tm=min(512, m), tn=tn, col_scales=(qk_scales, None, sq_scales, None),
        transposed=(False, True, False, False), tk=tq, name="even_in_proj")
    a_out = _diff_attn(qk.reshape(batch, seq, -1), vt, lq1, lk1, lq2, lk2, sub_g, batch=batch,
                       seq=seq, tq=tq, lambda_init=lambda_init, name="diff_attn")
    b_out = _swa(sq.reshape(batch, seq, -1), skv.reshape(batch, seq, -1), sinks,
                 batch=batch, seq=seq, name="swa")
    w_o = w_out.astype(BF16)
    return _out_proj([a_out.reshape(m, n_v), b_out.reshape(m, n_sq)],
                     [w_o[:n_v], w_o[n_v:]], xf, g_post, tm=min(512, m), sub=256,
                     name="even_out_proj")


def _odd_mixer(xf, g_pre, w_in, conv_w, conv_b, gate_w, gate_b, lru_lambda, w_out, g_post,
               *, batch, seq):
    m = batch * seq
    width = w_in.shape[1] // 2
    w_b = w_in.astype(BF16)
    yb, xb = _norm_proj(xf, g_pre, [w_b[:, :width], w_b[:, width:]],
                        [jax.nn.gelu, _identity], [BF16, F32],
                        tm=min(512, m), tn=512, name="odd_in_proj")
    hy = _rglru(xb, yb, conv_w, conv_b, (0.5 * gate_w).astype(BF16), 0.5 * gate_b, lru_lambda,
                batch=batch, seq=seq, chunk=min(512, seq), name="rglru")
    return _out_proj([hy.reshape(m, width)], [w_out.astype(BF16)], xf, g_post,
                     tm=min(512, m), sub=256, name="odd_out_proj")


def kernel(x, even_w_in, even_lam_q1, even_lam_k1, even_lam_q2, even_lam_k2, even_subln_g,
           even_sinks, even_w_out, odd_w_in, odd_conv_w, odd_conv_b, odd_gate_w, odd_gate_b,
           odd_lru_lambda, odd_w_out, pre_mix_g, post_mix_g, pre_mlp_g, post_mlp_g, mlp_w1,
           mlp_w2):
    batch, seq, d = x.shape
    m = batch * seq
    xf = x.reshape(m, d)
    depth = pre_mix_g.shape[0]
    w1_b = _cast_bf16(mlp_w1, name="cast_mlp_w1")
    w2_b = _cast_bf16(mlp_w2, name="cast_mlp_w2")
    for layer in range(depth):
        if layer % 2 == 0:
            e = layer // 2
            lambda_init = 0.8 - 0.6 * math.exp(-0.3 * layer)
            xf = _even_mixer(xf, pre_mix_g[layer], even_w_in[e], even_lam_q1[e],
                             even_lam_k1[e], even_lam_q2[e], even_lam_k2[e], even_subln_g[e],
                             even_sinks[e], even_w_out[e], post_mix_g[layer],
                             batch=batch, seq=seq, lambda_init=lambda_init)
        else:
            o = layer // 2
            xf = _odd_mixer(xf, pre_mix_g[layer], odd_w_in[o], odd_conv_w[o], odd_conv_b[o],
                            odd_gate_w[o], odd_gate_b[o], odd_lru_lambda[o], odd_w_out[o],
                            post_mix_g[layer], batch=batch, seq=seq)
        xf = _mlp(xf, pre_mlp_g[layer], w1_b, w2_b, post_mlp_g[layer], layer=layer,
                  tm=min(512, m), tf=1024, sub=min(256, m), name=f"mlp_{layer}")
    return xf.reshape(batch, seq, d)
```

```python
import functools
import math

import jax
import jax.numpy as jnp
from jax import lax
from jax.experimental import pallas as pl
from jax.experimental.pallas import tpu as pltpu

F32 = jnp.float32
BF16 = jnp.bfloat16

EPS = 1e-6
HEAD_DIM = 64
DIFF_HEADS = 8
SWA_Q_HEADS = 16
SWA_KV_HEADS = 2
SWA_GROUP = SWA_Q_HEADS // SWA_KV_HEADS
SWA_BLOCK = 128
LRU_BLOCKS = 8
CONV_WIDTH = 4
LRU_C = 8.0

LANES = 128
SUBLANES = 8
BF16_SUBLANES = 16
MIB = 1024 * 1024

NT_DIMS = (((1,), (1,)), ((), ()))


def _params(semantics, vmem_mib):
    return pltpu.CompilerParams(dimension_semantics=semantics,
                                vmem_limit_bytes=vmem_mib * MIB)


def _rmsnorm(x, g):
    ms = jnp.mean(x * x, axis=-1, keepdims=True)
    return x * lax.rsqrt(ms + EPS) * g


def _cast_rider_specs(arrays, n_steps, index_of_step):
    views, specs, shapes = [], [], []
    for a in arrays:
        total = a.shape[0] * a.shape[1]
        cols = a.shape[1]
        while (total // n_steps) // cols < BF16_SUBLANES:
            cols //= 2
        rows = total // n_steps // cols
        views.append(a.reshape(n_steps * rows, cols))
        specs.append(pl.BlockSpec((rows, cols), index_of_step))
        shapes.append(jax.ShapeDtypeStruct((n_steps * rows, cols), BF16))
    return views, specs, shapes


def _cast_riders(in_refs, out_refs):
    for src_ref, dst_ref in zip(in_refs, out_refs):
        dst_ref[...] = src_ref[...].astype(dst_ref.dtype)


def _resident(shape):
    return pl.BlockSpec(shape, lambda *_: (0,) * len(shape), pipeline_mode=pl.Buffered(1))


def _norm_proj_kernel(x_ref, g_ref, *refs, epilogues, col_scales, transposed, tn, tk, n_riders):
    n = len(epilogues)
    w_refs, rider_in = refs[:n], refs[n:n + n_riders]
    o_refs, rider_out = refs[n + n_riders:2 * n + n_riders], refs[2 * n + n_riders:]
    _cast_riders(rider_in, rider_out)
    x = x_ref[...]
    rs = lax.rsqrt(jnp.mean(x * x, axis=-1, keepdims=True) + EPS)
    xg = (x * g_ref[...]).astype(BF16)
    jobs = sorted((c0, k) for k in range(n) for c0 in range(0, w_refs[k].shape[1], tn))
    for c0, k in jobs:
        w_ref, o_ref, scales = w_refs[k], o_refs[k], col_scales[k]
        row_scale = rs if scales is None else rs * scales[c0 // tn]
        res = epilogues[k](
            jnp.dot(xg, w_ref[:, c0:c0 + tn], preferred_element_type=F32) * row_scale)
        if transposed[k]:
            res_t = res.T.astype(o_ref.dtype)
            for c in range(o_ref.shape[0]):
                o_ref[c, c0:c0 + tn, :] = res_t[:, c * tk:(c + 1) * tk]
        else:
            o_ref[:, c0:c0 + tn] = res.astype(o_ref.dtype)


def _norm_proj(x, g, ws, epilogues, out_dtypes, *, tm, tn, name, col_scales=None,
               transposed=None, tk=None, riders=()):
    m, d = x.shape
    rider_views, rider_specs, rider_shapes = _cast_rider_specs(riders, m // tm, lambda i: (i, 0))
    transposed = transposed or (False,) * len(ws)
    col_scales = col_scales or (None,) * len(ws)
    out_specs, out_shape = [], []
    for w, dt, tr in zip(ws, out_dtypes, transposed):
        n = w.shape[1]
        if tr:
            out_specs.append(pl.BlockSpec((tm // tk, n, tk), lambda i: (i, 0, 0)))
            out_shape.append(jax.ShapeDtypeStruct((m // tk, n, tk), dt))
        else:
            out_specs.append(pl.BlockSpec((tm, n), lambda i: (i, 0)))
            out_shape.append(jax.ShapeDtypeStruct((m, n), dt))
    return pl.pallas_call(
        functools.partial(_norm_proj_kernel, epilogues=tuple(epilogues),
                          col_scales=tuple(col_scales), transposed=tuple(transposed),
                          tn=tn, tk=tk, n_riders=len(riders)),
        grid=(m // tm,),
        in_specs=[pl.BlockSpec((tm, d), lambda i: (i, 0)), _resident((1, d))]
                 + [_resident(w.shape) for w in ws] + rider_specs,
        out_specs=out_specs + rider_specs,
        out_shape=out_shape + rider_shapes,
        compiler_params=_params(("parallel",), 56),
        name=name,
    )(x, g.reshape(1, d), *ws, *rider_views)


DIFF_HEADS_PER_STEP = 4
LOG2_E = 1.4426950408889634
QK_SCALE_LOG2 = HEAD_DIM ** -0.5 * LOG2_E
SUM_ROWS = 16


def _diff_attn_kernel(lq1_ref, lk1_ref, lq2_ref, lk2_ref, g_ref, q_ref, k_ref, vt_ref,
                      o_ref, *scratch, tq, lambda_init):
    hps = DIFF_HEADS_PER_STEP
    s_refs, qzt_refs, m_refs, acc_refs = (scratch[n * hps:(n + 1) * hps] for n in range(4))
    s_len = q_ref.shape[0]
    tk = tq
    heads = range(hps)
    last_head = hps - 1
    ones_rows = jnp.ones((SUM_ROWS, tk), BF16)
    lam = (jnp.exp(jnp.sum(lq1_ref[...] * lk1_ref[...], keepdims=True))
           - jnp.exp(jnp.sum(lq2_ref[...] * lk2_ref[...], keepdims=True)) + lambda_init)
    first_map = lax.broadcasted_iota(jnp.int32, (LANES, tq), 0) < HEAD_DIM
    kpos = lax.broadcasted_iota(jnp.int32, (tk, 2 * tq), 0)
    qcol = lax.broadcasted_iota(jnp.int32, (tk, 2 * tq), 1)
    causal = kpos <= jnp.where(qcol >= tq, qcol - tq, qcol)

    def scores(hh, j):
        cols = slice(hh * LANES, (hh + 1) * LANES)
        k = k_ref[pl.ds(pl.multiple_of(j * tk, tk), tk), cols]
        s_refs[hh][...] = jnp.dot(k, qzt_refs[hh][...], preferred_element_type=F32)

    def consume(hh, j, masked):
        cols = slice(hh * LANES, (hh + 1) * LANES)
        st = s_refs[hh][...]
        if masked:
            st = jnp.where(causal, st, -jnp.inf)
        m_old = m_refs[hh][...]
        m_new = jnp.maximum(m_old, jnp.max(st, axis=0, keepdims=True))
        alpha = jnp.exp2(m_old - m_new)
        p = jnp.exp2(st - m_new).astype(BF16)
        vt_ones = jnp.concatenate([vt_ref[j, cols, :], ones_rows], axis=0)
        acc_refs[hh][...] = alpha * acc_refs[hh][...] + jnp.dot(
            vt_ones, p, preferred_element_type=F32)
        m_refs[hh][...] = m_new

    def start_q_tile(i):
        r0 = pl.multiple_of(i * tq, tq)
        for hh in heads:
            qt = q_ref[pl.ds(r0, tq), hh * LANES:(hh + 1) * LANES].astype(F32).T
            qzt_refs[hh][:, 0:tq] = jnp.where(first_map, qt, 0.0).astype(BF16)
            qzt_refs[hh][:, tq:2 * tq] = jnp.where(first_map, 0.0, qt).astype(BF16)
            if hh != last_head:
                scores(hh, 0)

    def reset_state():
        for hh in heads:
            m_refs[hh][...] = jnp.full(m_refs[hh].shape, -jnp.inf, F32)
            acc_refs[hh][...] = jnp.zeros(acc_refs[hh].shape, F32)

    def step(j):
        scores(last_head, j)
        for hh in heads:
            consume(hh, j, False)
            if hh != last_head:
                scores(hh, j + 1)

    def finish_q_tile(i, odd):
        r0 = pl.multiple_of(i * tq, tq)
        if odd:
            step(i - 1)
        scores(last_head, i)
        for hh in heads:
            consume(hh, i, True)
        start_q_tile(jnp.minimum(i + 1, n_q - 1))
        for hh in heads:
            acc = acc_refs[hh][0:LANES, :]
            inv_l = 1.0 / acc_refs[hh][LANES:LANES + 1, :]
            ot = acc[:, :tq] * inv_l[:, :tq] - lam * (acc[:, tq:] * inv_l[:, tq:])
            o = _rmsnorm(ot.T, g_ref[...]) * (1.0 - lambda_init)
            o_ref[pl.ds(r0, tq), hh * LANES:(hh + 1) * LANES] = o.astype(o_ref.dtype)
        reset_state()

    def q_body(i, carry):
        def pair(jp, c):
            step(2 * jp)
            step(2 * jp + 1)
            return c

        lax.fori_loop(0, i // 2, pair, 0)
        pl.when(i % 2 == 1)(lambda: finish_q_tile(i, True))
        pl.when(i % 2 == 0)(lambda: finish_q_tile(i, False))
        return carry

    n_q = s_len // tq
    reset_state()
    start_q_tile(0)
    lax.fori_loop(0, n_q, q_body, 0)


def _diff_attn(proj, vt, lq1, lk1, lq2, lk2, sub_g, *, batch, seq, tq, lambda_init, name):
    hps = DIFF_HEADS_PER_STEP
    width = hps * LANES
    groups = DIFF_HEADS // hps
    vec = pl.BlockSpec((1, HEAD_DIM), lambda b, h: (0, 0))
    return pl.pallas_call(
        functools.partial(_diff_attn_kernel, tq=tq, lambda_init=lambda_init),
        grid=(batch, groups),
        in_specs=[vec, vec, vec, vec,
                  pl.BlockSpec((1, LANES), lambda b, h: (0, 0)),
                  pl.BlockSpec((None, seq, width), lambda b, h: (b, 0, h)),
                  pl.BlockSpec((None, seq, width), lambda b, h: (b, 0, groups + h)),
                  pl.BlockSpec((seq // tq, width, tq), lambda b, h: (b, h, 0))],
        out_specs=pl.BlockSpec((None, seq, width), lambda b, h: (b, 0, h)),
        out_shape=jax.ShapeDtypeStruct((batch, seq, DIFF_HEADS * LANES), BF16),
        scratch_shapes=[pltpu.VMEM((tq, 2 * tq), F32)] * hps
                       + [pltpu.VMEM((LANES, 2 * tq), BF16)] * hps
                       + [pltpu.VMEM((1, 2 * tq), F32)] * hps
                       + [pltpu.VMEM((LANES + SUM_ROWS, 2 * tq), F32)] * hps,
        compiler_params=_params(("parallel", "parallel"), 48),
        name=name,
    )(lq1.reshape(1, -1), lk1.reshape(1, -1), lq2.reshape(1, -1), lk2.reshape(1, -1),
      sub_g.reshape(1, -1), proj, proj, vt)


SWA_BLOCKS_PER_TRIP = 4


def _swa_kernel(sinks_ref, q_ref, kk_ref, vv_ref, o_ref):
    s_len = q_ref.shape[0]
    blk = SWA_BLOCK
    kvh = pl.program_id(1)
    first_head = lax.broadcasted_iota(jnp.int32, (blk, LANES), 1) < HEAD_DIM
    sinks = [sinks_ref[kvh * SWA_GROUP + g] * LOG2_E for g in range(SWA_GROUP)]

    nkeys = 2 * blk
    col_minus_row = (lax.broadcasted_iota(jnp.int32, (blk, nkeys), 1)
                     - lax.broadcasted_iota(jnp.int32, (blk, nkeys), 0))

    def block_scores(n):
        r0 = pl.multiple_of(n * blk, blk)
        k0 = pl.multiple_of(jnp.maximum(n - 1, 0) * blk, blk)
        qparts = []
        for gp in range(SWA_GROUP // 2):
            qq = q_ref[pl.ds(r0, blk), gp * LANES:(gp + 1) * LANES].astype(F32)
            qparts.append(jnp.where(first_head, qq, 0.0).astype(BF16))
            qparts.append(jnp.where(first_head, 0.0, qq).astype(BF16))
        qs = jnp.concatenate(qparts, axis=0)
        kk = kk_ref[pl.ds(k0, nkeys), :]
        return lax.dot_general(qs, kk, NT_DIMS, preferred_element_type=F32)

    def block_finish(n, s):
        r0 = pl.multiple_of(n * blk, blk)
        k0 = pl.multiple_of(jnp.maximum(n - 1, 0) * blk, blk)
        back = (r0 - k0) - col_minus_row
        valid = jnp.logical_and(back >= 0, back < blk)
        ps = []
        for g in range(SWA_GROUP):
            sg = jnp.where(valid, s[g * blk:(g + 1) * blk], -jnp.inf)
            m = jnp.maximum(jnp.max(sg, axis=-1, keepdims=True), sinks[g])
            e = jnp.exp2(sg - m)
            denom = jnp.sum(e, axis=-1, keepdims=True) + jnp.exp2(sinks[g] - m)
            ps.append((e * (1.0 / denom)).astype(BF16))
        p = jnp.concatenate(ps, axis=0)
        o = jnp.dot(p, vv_ref[pl.ds(k0, nkeys), :], preferred_element_type=F32)
        for gp in range(SWA_GROUP // 2):
            lo = o[(2 * gp) * blk:(2 * gp + 1) * blk]
            hi = o[(2 * gp + 1) * blk:(2 * gp + 2) * blk]
            o_ref[pl.ds(r0, blk), gp * LANES:(gp + 1) * LANES] = jnp.where(
                first_head, lo, hi).astype(o_ref.dtype)

    def body(gi, c):
        blocks = [gi * SWA_BLOCKS_PER_TRIP + u for u in range(SWA_BLOCKS_PER_TRIP)]
        scores = [block_scores(n) for n in blocks]
        for n, s in zip(blocks, scores):
            block_finish(n, s)
        return c

    lax.fori_loop(0, s_len // (blk * SWA_BLOCKS_PER_TRIP), body, 0)


def _swa(q, kv, sinks, *, batch, seq, name):
    qw = SWA_GROUP * HEAD_DIM
    return pl.pallas_call(
        _swa_kernel,
        grid=(batch, SWA_KV_HEADS),
        in_specs=[pl.BlockSpec(memory_space=pltpu.SMEM),
                  pl.BlockSpec((None, seq, qw), lambda b, h: (b, 0, h)),
                  pl.BlockSpec((None, seq, LANES), lambda b, h: (b, 0, h)),
                  pl.BlockSpec((None, seq, LANES), lambda b, h: (b, 0, SWA_KV_HEADS + h))],
        out_specs=pl.BlockSpec((None, seq, qw), lambda b, h: (b, 0, h)),
        out_shape=jax.ShapeDtypeStruct((batch, seq, SWA_Q_HEADS * HEAD_DIM), BF16),
        compiler_params=_params(("parallel", "parallel"), 40),
        name=name,
    )(sinks, q, kv, kv)


def _out_proj_kernel(*refs, n_in, sub):
    a_refs, w_refs = refs[:n_in], refs[n_in:2 * n_in]
    x_ref, g_ref, o_ref = refs[2 * n_in:]
    for r0 in range(0, x_ref.shape[0], sub):
        rows = slice(r0, r0 + sub)
        mix = jnp.dot(a_refs[0][rows, :], w_refs[0][...], preferred_element_type=F32)
        for a_ref, w_ref in zip(a_refs[1:], w_refs[1:]):
            mix = mix + jnp.dot(a_ref[rows, :], w_ref[...], preferred_element_type=F32)
        o_ref[rows, :] = x_ref[rows, :] + _rmsnorm(mix, g_ref[...])


def _out_proj(acts, ws, x, g, *, tm, sub, name):
    m, d = x.shape
    n_in = len(acts)
    return pl.pallas_call(
        functools.partial(_out_proj_kernel, n_in=n_in, sub=sub),
        grid=(m // tm,),
        in_specs=[pl.BlockSpec((tm, a.shape[1]), lambda i: (i, 0)) for a in acts]
                 + [_resident(w.shape) for w in ws]
                 + [pl.BlockSpec((tm, d), lambda i: (i, 0)), _resident((1, d))],
        out_specs=pl.BlockSpec((tm, d), lambda i: (i, 0)),
        out_shape=jax.ShapeDtypeStruct((m, d), F32),
        compiler_params=_params(("parallel",), 52),
        name=name,
    )(*acts, *ws, x, g.reshape(1, d))


def _mlp_kernel(x_ref, g1_ref, w1_ref, w2_ref, g2_ref, *refs, sub, n_riders):
    rider_in, o_ref = refs[:n_riders], refs[n_riders]
    rider_out = refs[n_riders + 1:2 * n_riders + 1]
    hn_ref, acc_ref = refs[2 * n_riders + 1:]
    _cast_riders(rider_in, rider_out)
    j = pl.program_id(1)
    last = pl.num_programs(1) - 1
    tm = x_ref.shape[0]

    def step(first, final, rows_per_pass):
        for r0 in range(0, tm, rows_per_pass):
            rows = slice(r0, r0 + rows_per_pass)
            if first:
                hn = _rmsnorm(x_ref[rows, :], g1_ref[...]).astype(BF16)
                hn_ref[rows, :] = hn
            else:
                hn = hn_ref[rows, :]
            z = jnp.maximum(jnp.dot(hn, w1_ref[...], preferred_element_type=F32), 0.0)
            part = jnp.dot((z * z).astype(BF16), w2_ref[...], preferred_element_type=F32)
            acc = part if first else acc_ref[rows, :] + part
            if final:
                o_ref[rows, :] = x_ref[rows, :] + _rmsnorm(acc, g2_ref[...])
            else:
                acc_ref[rows, :] = acc

    pl.when(j == 0)(lambda: step(True, False, sub))
    pl.when(jnp.logical_and(j > 0, j < last))(lambda: step(False, False, tm))
    pl.when(j == last)(lambda: step(False, True, sub))


def _mlp(x, g1, w1, w2, g2, *, tm, tf, sub, name, riders=()):
    m, d = x.shape
    f = w1.shape[1]
    steps = f // tf
    assert steps >= 2, "the first and the last d_ff step must be different steps"
    rider_views, rider_specs, rider_shapes = _cast_rider_specs(
        riders, (m // tm) * steps, lambda i, j: (i * steps + j, 0))
    return pl.pallas_call(
        functools.partial(_mlp_kernel, sub=sub, n_riders=len(riders)),
        grid=(m // tm, steps),
        in_specs=[pl.BlockSpec((tm, d), lambda i, j: (i, 0)),
                  _resident((1, d)),
                  pl.BlockSpec((d, tf), lambda i, j: (0, j)),
                  pl.BlockSpec((tf, d), lambda i, j: (j, 0)),
                  _resident((1, d))] + rider_specs,
        out_specs=[pl.BlockSpec((tm, d), lambda i, j: (i, 0))] + rider_specs,
        out_shape=[jax.ShapeDtypeStruct((m, d), F32)] + rider_shapes,
        scratch_shapes=[pltpu.VMEM((tm, d), BF16), pltpu.VMEM((tm, d), F32)],
        compiler_params=_params(("parallel", "arbitrary"), 52),
        name=name,
    )(x, g1.reshape(1, d), w1, w2, g2.reshape(1, d), *rider_views)


def _rglru_kernel(x_ref, y_ref, cw_ref, cb_ref, wr_ref, wi_ref, br_ref, bi_ref, lam_ref,
                  o_ref, xp_ref, *, chunk):
    s_len, c = x_ref.shape
    pad = SUBLANES
    xp_ref[0:pad, :] = jnp.zeros((pad, c), F32)
    xp_ref[pad:, :] = x_ref[...]
    log2_a_slope = jax.nn.softplus(-lam_ref[...]) * (-0.5 * LRU_C * LOG2_E)
    cw = cw_ref[...]
    groups = chunk // SUBLANES
    row_in_group = lax.broadcasted_iota(jnp.int32, (groups, SUBLANES, c), 1)

    def chunk_body(ci, h):
        t0 = ci * chunk
        own = xp_ref[t0 + pad:t0 + pad + chunk, :].reshape(groups, SUBLANES, c)
        prev = xp_ref[t0:t0 + chunk, :].reshape(groups, SUBLANES, c)
        xc = cb_ref[...] + cw[CONV_WIDTH - 1:CONV_WIDTH] * own
        for d in range(1, CONV_WIDTH):
            tap = cw[CONV_WIDTH - 1 - d:CONV_WIDTH - d]
            mixed = jnp.where(row_in_group >= SUBLANES - d, prev, own)
            xc = xc + tap * pltpu.roll(mixed, d, 1)
        xc = xc.reshape(chunk, c)
        xcb = xc.astype(BF16)
        tanh_r = jnp.tanh(jnp.dot(xcb, wr_ref[...], preferred_element_type=F32) + br_ref[...])
        tanh_i = jnp.tanh(jnp.dot(xcb, wi_ref[...], preferred_element_type=F32) + bi_ref[...])
        a = jnp.exp2(log2_a_slope * (tanh_r + 1.0))
        y = 1.0 - a * a
        gated_x = (0.5 * xc) * (tanh_i + 1.0)
        u = jnp.where(y > 0.0, y * lax.rsqrt(y), 0.0) * gated_x
        a = a.reshape(groups, SUBLANES, c)
        u = u.reshape(groups, SUBLANES, c)
        for d in (1, 2, 4):
            keep = row_in_group >= d
            a_prev = jnp.where(keep, pltpu.roll(a, d, 1), 1.0)
            u_prev = jnp.where(keep, pltpu.roll(u, d, 1), 0.0)
            u = a * u_prev + u
            a = a * a_prev
        states = []
        for g in range(groups):
            hv = a[g] * h + u[g]
            states.append(hv)
            h = jnp.broadcast_to(hv[SUBLANES - 1:SUBLANES, :], (SUBLANES, c))
        hs = jnp.concatenate(states, axis=0)
        o_ref[t0:t0 + chunk, :] = (
            hs * y_ref[t0:t0 + chunk, :].astype(F32)).astype(o_ref.dtype)
        return h

    h = jnp.zeros((SUBLANES, c), F32)
    for ci in range(s_len // chunk):
        h = chunk_body(ci, h)


def _rglru(xb, yb, conv_w, conv_b, gate_w, gate_b, lru_lambda, *, batch, seq, chunk, name):
    width = xb.shape[-1]
    c = width // LRU_BLOCKS
    row = pl.BlockSpec((1, c), lambda b, n: (0, n))
    gate = lambda g: pl.BlockSpec((None, None, c, c), lambda b, n: (g, n, 0, 0))
    gate_bias = lambda g: pl.BlockSpec((None, 1, c), lambda b, n: (g, 0, n))
    return pl.pallas_call(
        functools.partial(_rglru_kernel, chunk=chunk),
        grid=(batch, LRU_BLOCKS),
        in_specs=[pl.BlockSpec((None, seq, c), lambda b, n: (b, 0, n)),
                  pl.BlockSpec((None, seq, c), lambda b, n: (b, 0, n)),
                  pl.BlockSpec((CONV_WIDTH, c), lambda b, n: (0, n)),
                  row, gate(0), gate(1), gate_bias(0), gate_bias(1), row],
        out_specs=pl.BlockSpec((None, seq, c), lambda b, n: (b, 0, n)),
        out_shape=jax.ShapeDtypeStruct((batch, seq, width), BF16),
        scratch_shapes=[pltpu.VMEM((seq + SUBLANES, c), F32)],
        compiler_params=_params(("parallel", "parallel"), 40),
        name=name,
    )(xb.reshape(batch, seq, width), yb.reshape(batch, seq, width), conv_w,
      conv_b.reshape(1, width), gate_w, gate_w, gate_b.reshape(2, 1, width),
      gate_b.reshape(2, 1, width), lru_lambda.reshape(1, width))


def _identity(v):
    return v


def _even_mixer(xf, g_pre, w_in, lq1, lk1, lq2, lk2, sub_g, sinks, w_out, g_post,
                *, batch, seq, lambda_init, riders):
    m = batch * seq
    n_qk = 2 * DIFF_HEADS * HEAD_DIM
    n_v = DIFF_HEADS * 2 * HEAD_DIM
    n_sq = SWA_Q_HEADS * HEAD_DIM
    c0 = 2 * n_qk + n_v + n_sq
    ks = [w_in[:, c0 + h * HEAD_DIM:c0 + (h + 1) * HEAD_DIM] for h in range(SWA_KV_HEADS)]
    c1 = c0 + SWA_KV_HEADS * HEAD_DIM
    vs = [w_in[:, c1 + h * HEAD_DIM:c1 + (h + 1) * HEAD_DIM] for h in range(SWA_KV_HEADS)]
    w_qk = w_in[:, :2 * n_qk].astype(BF16)
    w_v = w_in[:, 2 * n_qk:2 * n_qk + n_v].astype(BF16)
    w_sq = w_in[:, 2 * n_qk + n_v:c0].astype(BF16)
    w_skv = jnp.concatenate([t for k in ks for t in (k, k)] + [t for v in vs for t in (v, v)],
                            axis=1).astype(BF16)

    tq = min(256, seq)
    tn = 512
    qk_scales = tuple(QK_SCALE_LOG2 if c < n_qk else 1.0 for c in range(0, 2 * n_qk, tn))
    sq_scales = (QK_SCALE_LOG2,) * (n_sq // tn)
    qk, vt, sq, skv, *cast = _norm_proj(
        xf, g_pre, [w_qk, w_v, w_sq, w_skv], [_identity] * 4, [BF16] * 4,
        tm=min(512, m), tn=tn, col_scales=(qk_scales, None, sq_scales, None),
        transposed=(False, True, False, False), tk=tq, riders=riders, name="even_in_proj")
    a_out = _diff_attn(qk.reshape(batch, seq, -1), vt, lq1, lk1, lq2, lk2, sub_g, batch=batch,
                       seq=seq, tq=tq, lambda_init=lambda_init, name="diff_attn")
    b_out = _swa(sq.reshape(batch, seq, -1), skv.reshape(batch, seq, -1), sinks,
                 batch=batch, seq=seq, name="swa")
    w_o = w_out.astype(BF16)
    out = _out_proj([a_out.reshape(m, n_v), b_out.reshape(m, n_sq)],
                    [w_o[:n_v], w_o[n_v:]], xf, g_post, tm=min(512, m), sub=256,
                    name="even_out_proj")
    return out, cast


def _odd_mixer(xf, g_pre, w_in, conv_w, conv_b, gate_w, gate_b, lru_lambda, w_out, g_post,
               *, batch, seq, riders):
    m = batch * seq
    width = w_in.shape[1] // 2
    w_b = w_in.astype(BF16)
    yb, xb, *cast = _norm_proj(xf, g_pre, [w_b[:, :width], w_b[:, width:]],
                               [jax.nn.gelu, _identity], [BF16, F32],
                               tm=min(512, m), tn=512, riders=riders, name="odd_in_proj")
    hy = _rglru(xb, yb, conv_w, conv_b, (0.5 * gate_w).astype(BF16), 0.5 * gate_b, lru_lambda,
                batch=batch, seq=seq, chunk=min(512, seq), name="rglru")
    out = _out_proj([hy.reshape(m, width)], [w_out.astype(BF16)], xf, g_post,
                    tm=min(512, m), sub=256, name="odd_out_proj")
    return out, cast


def kernel(x, even_w_in, even_lam_q1, even_lam_k1, even_lam_q2, even_lam_k2, even_subln_g,
           even_sinks, even_w_out, odd_w_in, odd_conv_w, odd_conv_b, odd_gate_w, odd_gate_b,
           odd_lru_lambda, odd_w_out, pre_mix_g, post_mix_g, pre_mlp_g, post_mlp_g, mlp_w1,
           mlp_w2):
    batch, seq, d = x.shape
    m = batch * seq
    xf = x.reshape(m, d)
    depth = pre_mix_g.shape[0]
    riders = (mlp_w1[0], mlp_w2[0])
    for layer in range(depth):
        if layer % 2 == 0:
            e = layer // 2
            lambda_init = 0.8 - 0.6 * math.exp(-0.3 * layer)
            xf, cast = _even_mixer(xf, pre_mix_g[layer], even_w_in[e], even_lam_q1[e],
                                   even_lam_k1[e], even_lam_q2[e], even_lam_k2[e],
                                   even_subln_g[e], even_sinks[e], even_w_out[e],
                                   post_mix_g[layer], batch=batch, seq=seq,
                                   lambda_init=lambda_init, riders=riders)
        else:
            o = layer // 2
            xf, cast = _odd_mixer(xf, pre_mix_g[layer], odd_w_in[o], odd_conv_w[o],
                                  odd_conv_b[o], odd_gate_w[o], odd_gate_b[o],
                                  odd_lru_lambda[o], odd_w_out[o], post_mix_g[layer],
                                  batch=batch, seq=seq, riders=riders)
        if riders:
            w1_b = cast[0].reshape(mlp_w1.shape[1:])
            w2_b = cast[1].reshape(mlp_w2.shape[1:])
        riders = (mlp_w1[layer + 1], mlp_w2[layer + 1]) if layer + 1 < depth else ()
        xf, *cast = _mlp(xf, pre_mlp_g[layer], w1_b, w2_b, post_mlp_g[layer],
                         tm=min(512, m), tf=1024, sub=min(256, m), riders=riders,
                         name=f"mlp_{layer}")
        if riders:
            w1_b = cast[0].reshape(mlp_w1.shape[1:])
            w2_b = cast[1].reshape(mlp_w2.shape[1:])
            riders = ()
    return xf.reshape(batch, seq, d)
```

```python
import functools
import math

import jax
import jax.numpy as jnp
from jax import lax
from jax.experimental import pallas as pl
from jax.experimental.pallas import tpu as pltpu

F32 = jnp.float32
BF16 = jnp.bfloat16

EPS = 1e-6
HEAD_DIM = 64
DIFF_HEADS = 8
SWA_Q_HEADS = 16
SWA_KV_HEADS = 2
SWA_GROUP = SWA_Q_HEADS // SWA_KV_HEADS
SWA_BLOCK = 128
LRU_BLOCKS = 8
CONV_WIDTH = 4
LRU_C = 8.0

LANES = 128
SUBLANES = 8
BF16_SUBLANES = 16
MIB = 1024 * 1024

NT_DIMS = (((1,), (1,)), ((), ()))


def _params(semantics, vmem_mib):
    return pltpu.CompilerParams(dimension_semantics=semantics,
                                vmem_limit_bytes=vmem_mib * MIB)


def _rmsnorm(x, g):
    ms = jnp.mean(x * x, axis=-1, keepdims=True)
    return x * lax.rsqrt(ms + EPS) * g


def _cast_rider_specs(riders, n_steps):
    in_specs, out_specs, shapes = [], [], []
    for stack, layer in riders:
        _, rows, cols = stack.shape
        block_rows = rows // n_steps
        assert block_rows % BF16_SUBLANES == 0, "rider blocks must be whole bf16 row tiles"
        in_specs.append(pl.BlockSpec((None, block_rows, cols),
                                     functools.partial(lambda i, l: (l, i, 0), l=layer)))
        out_specs.append(pl.BlockSpec((block_rows, cols), lambda i: (i, 0)))
        shapes.append(jax.ShapeDtypeStruct((rows, cols), BF16))
    return in_specs, out_specs, shapes


def _cast_riders(in_refs, out_refs):
    for src_ref, dst_ref in zip(in_refs, out_refs):
        dst_ref[...] = src_ref[...].astype(dst_ref.dtype)


def _resident(shape):
    return pl.BlockSpec(shape, lambda *_: (0,) * len(shape), pipeline_mode=pl.Buffered(1))


def _norm_proj_kernel(x_ref, g_ref, *refs, epilogues, col_scales, transposed, tn, tk, n_riders):
    n = len(epilogues)
    w_refs, rider_in = refs[:n], refs[n:n + n_riders]
    o_refs, rider_out = refs[n + n_riders:2 * n + n_riders], refs[2 * n + n_riders:]
    _cast_riders(rider_in, rider_out)
    x = x_ref[...]
    rs = lax.rsqrt(jnp.mean(x * x, axis=-1, keepdims=True) + EPS)
    xg = (x * g_ref[...]).astype(BF16)
    jobs = sorted((c0, k) for k in range(n) for c0 in range(0, w_refs[k].shape[1], tn))
    for c0, k in jobs:
        w_ref, o_ref, scales = w_refs[k], o_refs[k], col_scales[k]
        row_scale = rs if scales is None else rs * scales[c0 // tn]
        res = epilogues[k](
            jnp.dot(xg, w_ref[:, c0:c0 + tn], preferred_element_type=F32) * row_scale)
        if transposed[k]:
            res_t = res.T.astype(o_ref.dtype)
            for c in range(o_ref.shape[0]):
                o_ref[c, c0:c0 + tn, :] = res_t[:, c * tk:(c + 1) * tk]
        else:
            o_ref[:, c0:c0 + tn] = res.astype(o_ref.dtype)


def _norm_proj(x, g, ws, epilogues, out_dtypes, *, tm, tn, name, col_scales=None,
               transposed=None, tk=None, riders=()):
    m, d = x.shape
    rider_in_specs, rider_out_specs, rider_shapes = _cast_rider_specs(riders, m // tm)
    transposed = transposed or (False,) * len(ws)
    col_scales = col_scales or (None,) * len(ws)
    out_specs, out_shape = [], []
    for w, dt, tr in zip(ws, out_dtypes, transposed):
        n = w.shape[1]
        if tr:
            out_specs.append(pl.BlockSpec((tm // tk, n, tk), lambda i: (i, 0, 0)))
            out_shape.append(jax.ShapeDtypeStruct((m // tk, n, tk), dt))
        else:
            out_specs.append(pl.BlockSpec((tm, n), lambda i: (i, 0)))
            out_shape.append(jax.ShapeDtypeStruct((m, n), dt))
    return pl.pallas_call(
        functools.partial(_norm_proj_kernel, epilogues=tuple(epilogues),
                          col_scales=tuple(col_scales), transposed=tuple(transposed),
                          tn=tn, tk=tk, n_riders=len(riders)),
        grid=(m // tm,),
        in_specs=[pl.BlockSpec((tm, d), lambda i: (i, 0)), _resident((1, d))]
                 + [_resident(w.shape) for w in ws] + rider_in_specs,
        out_specs=out_specs + rider_out_specs,
        out_shape=out_shape + rider_shapes,
        compiler_params=_params(("parallel",), 56),
        name=name,
    )(x, g.reshape(1, d), *ws, *[stack for stack, _ in riders])


DIFF_HEADS_PER_STEP = 4
LOG2_E = 1.4426950408889634
QK_SCALE_LOG2 = HEAD_DIM ** -0.5 * LOG2_E
SUM_ROWS = 16


def _diff_attn_kernel(lq1_ref, lk1_ref, lq2_ref, lk2_ref, g_ref, q_ref, k_ref, vt_ref,
                      o_ref, *scratch, tq, lambda_init):
    hps = DIFF_HEADS_PER_STEP
    s_refs, qzt_refs, m_refs, acc_refs = (scratch[n * hps:(n + 1) * hps] for n in range(4))
    s_len = q_ref.shape[0]
    tk = tq
    heads = range(hps)
    last_head = hps - 1
    ones_rows = jnp.ones((SUM_ROWS, tk), BF16)
    lam = (jnp.exp(jnp.sum(lq1_ref[...] * lk1_ref[...], keepdims=True))
           - jnp.exp(jnp.sum(lq2_ref[...] * lk2_ref[...], keepdims=True)) + lambda_init)
    first_map = lax.broadcasted_iota(jnp.int32, (LANES, tq), 0) < HEAD_DIM
    kpos = lax.broadcasted_iota(jnp.int32, (tk, 2 * tq), 0)
    qcol = lax.broadcasted_iota(jnp.int32, (tk, 2 * tq), 1)
    causal = kpos <= jnp.where(qcol >= tq, qcol - tq, qcol)

    def scores(hh, j):
        cols = slice(hh * LANES, (hh + 1) * LANES)
        k = k_ref[pl.ds(pl.multiple_of(j * tk, tk), tk), cols]
        s_refs[hh][...] = jnp.dot(k, qzt_refs[hh][...], preferred_element_type=F32)

    def consume(hh, j, masked):
        cols = slice(hh * LANES, (hh + 1) * LANES)
        st = s_refs[hh][...]
        if masked:
            st = jnp.where(causal, st, -jnp.inf)
        m_old = m_refs[hh][...]
        m_new = jnp.maximum(m_old, jnp.max(st, axis=0, keepdims=True))
        alpha = jnp.exp2(m_old - m_new)
        p = jnp.exp2(st - m_new).astype(BF16)
        vt_ones = jnp.concatenate([vt_ref[j, cols, :], ones_rows], axis=0)
        acc_refs[hh][...] = alpha * acc_refs[hh][...] + jnp.dot(
            vt_ones, p, preferred_element_type=F32)
        m_refs[hh][...] = m_new

    def start_q_tile(i):
        r0 = pl.multiple_of(i * tq, tq)
        for hh in heads:
            qt = q_ref[pl.ds(r0, tq), hh * LANES:(hh + 1) * LANES].astype(F32).T
            qzt_refs[hh][:, 0:tq] = jnp.where(first_map, qt, 0.0).astype(BF16)
            qzt_refs[hh][:, tq:2 * tq] = jnp.where(first_map, 0.0, qt).astype(BF16)
            if hh != last_head:
                scores(hh, 0)

    def reset_state():
        for hh in heads:
            m_refs[hh][...] = jnp.full(m_refs[hh].shape, -jnp.inf, F32)
            acc_refs[hh][...] = jnp.zeros(acc_refs[hh].shape, F32)

    def step(j):
        scores(last_head, j)
        for hh in heads:
            consume(hh, j, False)
            if hh != last_head:
                scores(hh, j + 1)

    def finish_q_tile(i, odd):
        r0 = pl.multiple_of(i * tq, tq)
        if odd:
            step(i - 1)
        scores(last_head, i)
        for hh in heads:
            consume(hh, i, True)
        start_q_tile(jnp.minimum(i + 1, n_q - 1))
        for hh in heads:
            acc = acc_refs[hh][0:LANES, :]
            inv_l = 1.0 / acc_refs[hh][LANES:LANES + 1, :]
            ot = acc[:, :tq] * inv_l[:, :tq] - lam * (acc[:, tq:] * inv_l[:, tq:])
            o = _rmsnorm(ot.T, g_ref[...]) * (1.0 - lambda_init)
            o_ref[pl.ds(r0, tq), hh * LANES:(hh + 1) * LANES] = o.astype(o_ref.dtype)
        reset_state()

    def q_body(i, carry):
        def pair(jp, c):
            step(2 * jp)
            step(2 * jp + 1)
            return c

        lax.fori_loop(0, i // 2, pair, 0)
        pl.when(i % 2 == 1)(lambda: finish_q_tile(i, True))
        pl.when(i % 2 == 0)(lambda: finish_q_tile(i, False))
        return carry

    n_q = s_len // tq
    reset_state()
    start_q_tile(0)
    lax.fori_loop(0, n_q, q_body, 0)


def _diff_attn(proj, vt, lq1, lk1, lq2, lk2, sub_g, *, batch, seq, tq, lambda_init, name):
    hps = DIFF_HEADS_PER_STEP
    width = hps * LANES
    groups = DIFF_HEADS // hps
    vec = pl.BlockSpec((1, HEAD_DIM), lambda b, h: (0, 0))
    return pl.pallas_call(
        functools.partial(_diff_attn_kernel, tq=tq, lambda_init=lambda_init),
        grid=(batch, groups),
        in_specs=[vec, vec, vec, vec,
                  pl.BlockSpec((1, LANES), lambda b, h: (0, 0)),
                  pl.BlockSpec((None, seq, width), lambda b, h: (b, 0, h)),
                  pl.BlockSpec((None, seq, width), lambda b, h: (b, 0, groups + h)),
                  pl.BlockSpec((seq // tq, width, tq), lambda b, h: (b, h, 0))],
        out_specs=pl.BlockSpec((None, seq, width), lambda b, h: (b, 0, h)),
        out_shape=jax.ShapeDtypeStruct((batch, seq, DIFF_HEADS * LANES), BF16),
        scratch_shapes=[pltpu.VMEM((tq, 2 * tq), F32)] * hps
                       + [pltpu.VMEM((LANES, 2 * tq), BF16)] * hps
                       + [pltpu.VMEM((1, 2 * tq), F32)] * hps
                       + [pltpu.VMEM((LANES + SUM_ROWS, 2 * tq), F32)] * hps,
        compiler_params=_params(("parallel", "parallel"), 48),
        name=name,
    )(lq1.reshape(1, -1), lk1.reshape(1, -1), lq2.reshape(1, -1), lk2.reshape(1, -1),
      sub_g.reshape(1, -1), proj, proj, vt)


SWA_BLOCKS_PER_TRIP = 4


def _swa_kernel(sinks_ref, q_ref, kk_ref, vv_ref, o_ref):
    s_len = q_ref.shape[0]
    blk = SWA_BLOCK
    kvh = pl.program_id(1)
    first_head = lax.broadcasted_iota(jnp.int32, (blk, LANES), 1) < HEAD_DIM
    sinks = [sinks_ref[kvh * SWA_GROUP + g] * LOG2_E for g in range(SWA_GROUP)]

    nkeys = 2 * blk
    col_minus_row = (lax.broadcasted_iota(jnp.int32, (blk, nkeys), 1)
                     - lax.broadcasted_iota(jnp.int32, (blk, nkeys), 0))

    def block_scores(n):
        r0 = pl.multiple_of(n * blk, blk)
        k0 = pl.multiple_of(jnp.maximum(n - 1, 0) * blk, blk)
        qparts = []
        for gp in range(SWA_GROUP // 2):
            qq = q_ref[pl.ds(r0, blk), gp * LANES:(gp + 1) * LANES].astype(F32)
            qparts.append(jnp.where(first_head, qq, 0.0).astype(BF16))
            qparts.append(jnp.where(first_head, 0.0, qq).astype(BF16))
        qs = jnp.concatenate(qparts, axis=0)
        kk = kk_ref[pl.ds(k0, nkeys), :]
        return lax.dot_general(qs, kk, NT_DIMS, preferred_element_type=F32)

    def block_finish(n, s):
        r0 = pl.multiple_of(n * blk, blk)
        k0 = pl.multiple_of(jnp.maximum(n - 1, 0) * blk, blk)
        back = (r0 - k0) - col_minus_row
        valid = jnp.logical_and(back >= 0, back < blk)
        ps = []
        for g in range(SWA_GROUP):
            sg = jnp.where(valid, s[g * blk:(g + 1) * blk], -jnp.inf)
            m = jnp.maximum(jnp.max(sg, axis=-1, keepdims=True), sinks[g])
            e = jnp.exp2(sg - m)
            denom = jnp.sum(e, axis=-1, keepdims=True) + jnp.exp2(sinks[g] - m)
            ps.append((e * (1.0 / denom)).astype(BF16))
        p = jnp.concatenate(ps, axis=0)
        o = jnp.dot(p, vv_ref[pl.ds(k0, nkeys), :], preferred_element_type=F32)
        for gp in range(SWA_GROUP // 2):
            lo = o[(2 * gp) * blk:(2 * gp + 1) * blk]
            hi = o[(2 * gp + 1) * blk:(2 * gp + 2) * blk]
            o_ref[pl.ds(r0, blk), gp * LANES:(gp + 1) * LANES] = jnp.where(
                first_head, lo, hi).astype(o_ref.dtype)

    def body(gi, c):
        blocks = [gi * SWA_BLOCKS_PER_TRIP + u for u in range(SWA_BLOCKS_PER_TRIP)]
        scores = [block_scores(n) for n in blocks]
        for n, s in zip(blocks, scores):
            block_finish(n, s)
        return c

    lax.fori_loop(0, s_len // (blk * SWA_BLOCKS_PER_TRIP), body, 0)


def _swa(q, kv, sinks, *, batch, seq, name):
    qw = SWA_GROUP * HEAD_DIM
    return pl.pallas_call(
        _swa_kernel,
        grid=(batch, SWA_KV_HEADS),
        in_specs=[pl.BlockSpec(memory_space=pltpu.SMEM),
                  pl.BlockSpec((None, seq, qw), lambda b, h: (b, 0, h)),
                  pl.BlockSpec((None, seq, LANES), lambda b, h: (b, 0, h)),
                  pl.BlockSpec((None, seq, LANES), lambda b, h: (b, 0, SWA_KV_HEADS + h))],
        out_specs=pl.BlockSpec((None, seq, qw), lambda b, h: (b, 0, h)),
        out_shape=jax.ShapeDtypeStruct((batch, seq, SWA_Q_HEADS * HEAD_DIM), BF16),
        compiler_params=_params(("parallel", "parallel"), 40),
        name=name,
    )(sinks, q, kv, kv)


def _out_proj_kernel(*refs, n_in, sub):
    a_refs, w_refs = refs[:n_in], refs[n_in:2 * n_in]
    x_ref, g_ref, o_ref = refs[2 * n_in:]
    for r0 in range(0, x_ref.shape[0], sub):
        rows = slice(r0, r0 + sub)
        mix = jnp.dot(a_refs[0][rows, :], w_refs[0][...], preferred_element_type=F32)
        for a_ref, w_ref in zip(a_refs[1:], w_refs[1:]):
            mix = mix + jnp.dot(a_ref[rows, :], w_ref[...], preferred_element_type=F32)
        o_ref[rows, :] = x_ref[rows, :] + _rmsnorm(mix, g_ref[...])


def _out_proj(acts, ws, x, g, *, tm, sub, name):
    m, d = x.shape
    n_in = len(acts)
    return pl.pallas_call(
        functools.partial(_out_proj_kernel, n_in=n_in, sub=sub),
        grid=(m // tm,),
        in_specs=[pl.BlockSpec((tm, a.shape[1]), lambda i: (i, 0)) for a in acts]
                 + [_resident(w.shape) for w in ws]
                 + [pl.BlockSpec((tm, d), lambda i: (i, 0)), _resident((1, d))],
        out_specs=pl.BlockSpec((tm, d), lambda i: (i, 0)),
        out_shape=jax.ShapeDtypeStruct((m, d), F32),
        compiler_params=_params(("parallel",), 52),
        name=name,
    )(*acts, *ws, x, g.reshape(1, d))


def _mlp_kernel(x_ref, g1_ref, w1_ref, w2_ref, g2_ref, o_ref, hn_ref, acc_ref, *, sub):
    j = pl.program_id(1)
    last = pl.num_programs(1) - 1
    tm = x_ref.shape[0]

    def step(first, final, rows_per_pass):
        for r0 in range(0, tm, rows_per_pass):
            rows = slice(r0, r0 + rows_per_pass)
            if first:
                hn = _rmsnorm(x_ref[rows, :], g1_ref[...]).astype(BF16)
                hn_ref[rows, :] = hn
            else:
                hn = hn_ref[rows, :]
            z = jnp.maximum(jnp.dot(hn, w1_ref[...], preferred_element_type=F32), 0.0)
            part = jnp.dot((z * z).astype(BF16), w2_ref[...], preferred_element_type=F32)
            acc = part if first else acc_ref[rows, :] + part
            if final:
                o_ref[rows, :] = x_ref[rows, :] + _rmsnorm(acc, g2_ref[...])
            else:
                acc_ref[rows, :] = acc

    pl.when(j == 0)(lambda: step(True, False, sub))
    pl.when(jnp.logical_and(j > 0, j < last))(lambda: step(False, False, tm))
    pl.when(j == last)(lambda: step(False, True, sub))


def _mlp(x, g1, w1, w2, g2, *, tm, tf, sub, name):
    m, d = x.shape
    f = w1.shape[1]
    assert f // tf >= 2, "the first and the last d_ff step must be different steps"
    return pl.pallas_call(
        functools.partial(_mlp_kernel, sub=sub),
        grid=(m // tm, f // tf),
        in_specs=[pl.BlockSpec((tm, d), lambda i, j: (i, 0)),
                  _resident((1, d)),
                  pl.BlockSpec((d, tf), lambda i, j: (0, j)),
                  pl.BlockSpec((tf, d), lambda i, j: (j, 0)),
                  _resident((1, d))],
        out_specs=pl.BlockSpec((tm, d), lambda i, j: (i, 0)),
        out_shape=jax.ShapeDtypeStruct((m, d), F32),
        scratch_shapes=[pltpu.VMEM((tm, d), BF16), pltpu.VMEM((tm, d), F32)],
        compiler_params=_params(("parallel", "arbitrary"), 52),
        name=name,
    )(x, g1.reshape(1, d), w1, w2, g2.reshape(1, d))


def _rglru_kernel(x_ref, y_ref, cw_ref, cb_ref, wr_ref, wi_ref, br_ref, bi_ref, lam_ref,
                  o_ref, xp_ref, *, chunk):
    s_len, c = x_ref.shape
    pad = SUBLANES
    xp_ref[0:pad, :] = jnp.zeros((pad, c), F32)
    xp_ref[pad:, :] = x_ref[...]
    log2_a_slope = jax.nn.softplus(-lam_ref[...]) * (-0.5 * LRU_C * LOG2_E)
    cw = cw_ref[...]
    groups = chunk // SUBLANES
    row_in_group = lax.broadcasted_iota(jnp.int32, (groups, SUBLANES, c), 1)

    def chunk_body(ci, h):
        t0 = ci * chunk
        own = xp_ref[t0 + pad:t0 + pad + chunk, :].reshape(groups, SUBLANES, c)
        prev = xp_ref[t0:t0 + chunk, :].reshape(groups, SUBLANES, c)
        xc = cb_ref[...] + cw[CONV_WIDTH - 1:CONV_WIDTH] * own
        for d in range(1, CONV_WIDTH):
            tap = cw[CONV_WIDTH - 1 - d:CONV_WIDTH - d]
            mixed = jnp.where(row_in_group >= SUBLANES - d, prev, own)
            xc = xc + tap * pltpu.roll(mixed, d, 1)
        xc = xc.reshape(chunk, c)
        xcb = xc.astype(BF16)
        tanh_r = jnp.tanh(jnp.dot(xcb, wr_ref[...], preferred_element_type=F32) + br_ref[...])
        tanh_i = jnp.tanh(jnp.dot(xcb, wi_ref[...], preferred_element_type=F32) + bi_ref[...])
        a = jnp.exp2(log2_a_slope * (tanh_r + 1.0))
        y = 1.0 - a * a
        gated_x = (0.5 * xc) * (tanh_i + 1.0)
        u = jnp.where(y > 0.0, y * lax.rsqrt(y), 0.0) * gated_x
        a = a.reshape(groups, SUBLANES, c)
        u = u.reshape(groups, SUBLANES, c)
        for d in (1, 2, 4):
            keep = row_in_group >= d
            a_prev = jnp.where(keep, pltpu.roll(a, d, 1), 1.0)
            u_prev = jnp.where(keep, pltpu.roll(u, d, 1), 0.0)
            u = a * u_prev + u
            a = a * a_prev
        states = []
        for g in range(groups):
            hv = a[g] * h + u[g]
            states.append(hv)
            h = jnp.broadcast_to(hv[SUBLANES - 1:SUBLANES, :], (SUBLANES, c))
        hs = jnp.concatenate(states, axis=0)
        o_ref[t0:t0 + chunk, :] = (
            hs * y_ref[t0:t0 + chunk, :].astype(F32)).astype(o_ref.dtype)
        return h

    h = jnp.zeros((SUBLANES, c), F32)
    for ci in range(s_len // chunk):
        h = chunk_body(ci, h)


def _rglru(xb, yb, conv_w, conv_b, gate_w, gate_b, lru_lambda, *, batch, seq, chunk, name):
    width = xb.shape[-1]
    c = width // LRU_BLOCKS
    row = pl.BlockSpec((1, c), lambda b, n: (0, n))
    gate = lambda g: pl.BlockSpec((None, None, c, c), lambda b, n: (g, n, 0, 0))
    gate_bias = lambda g: pl.BlockSpec((None, 1, c), lambda b, n: (g, 0, n))
    return pl.pallas_call(
        functools.partial(_rglru_kernel, chunk=chunk),
        grid=(batch, LRU_BLOCKS),
        in_specs=[pl.BlockSpec((None, seq, c), lambda b, n: (b, 0, n)),
                  pl.BlockSpec((None, seq, c), lambda b, n: (b, 0, n)),
                  pl.BlockSpec((CONV_WIDTH, c), lambda b, n: (0, n)),
                  row, gate(0), gate(1), gate_bias(0), gate_bias(1), row],
        out_specs=pl.BlockSpec((None, seq, c), lambda b, n: (b, 0, n)),
        out_shape=jax.ShapeDtypeStruct((batch, seq, width), BF16),
        scratch_shapes=[pltpu.VMEM((seq + SUBLANES, c), F32)],
        compiler_params=_params(("parallel", "parallel"), 40),
        name=name,
    )(xb.reshape(batch, seq, width), yb.reshape(batch, seq, width), conv_w,
      conv_b.reshape(1, width), gate_w, gate_w, gate_b.reshape(2, 1, width),
      gate_b.reshape(2, 1, width), lru_lambda.reshape(1, width))


def _identity(v):
    return v


def _even_mixer(xf, g_pre, w_in, lq1, lk1, lq2, lk2, sub_g, sinks, w_out, g_post,
                *, batch, seq, lambda_init, riders):
    m = batch * seq
    n_qk = 2 * DIFF_HEADS * HEAD_DIM
    n_v = DIFF_HEADS * 2 * HEAD_DIM
    n_sq = SWA_Q_HEADS * HEAD_DIM
    c0 = 2 * n_qk + n_v + n_sq
    ks = [w_in[:, c0 + h * HEAD_DIM:c0 + (h + 1) * HEAD_DIM] for h in range(SWA_KV_HEADS)]
    c1 = c0 + SWA_KV_HEADS * HEAD_DIM
    vs = [w_in[:, c1 + h * HEAD_DIM:c1 + (h + 1) * HEAD_DIM] for h in range(SWA_KV_HEADS)]
    w_qk = w_in[:, :2 * n_qk].astype(BF16)
    w_v = w_in[:, 2 * n_qk:2 * n_qk + n_v].astype(BF16)
    w_sq = w_in[:, 2 * n_qk + n_v:c0].astype(BF16)
    w_skv = jnp.concatenate([t for k in ks for t in (k, k)] + [t for v in vs for t in (v, v)],
                            axis=1).astype(BF16)

    tq = min(256, seq)
    tn = 512
    qk_scales = tuple(QK_SCALE_LOG2 if c < n_qk else 1.0 for c in range(0, 2 * n_qk, tn))
    sq_scales = (QK_SCALE_LOG2,) * (n_sq // tn)
    qk, vt, sq, skv, *cast = _norm_proj(
        xf, g_pre, [w_qk, w_v, w_sq, w_skv], [_identity] * 4, [BF16] * 4,
        tm=min(512, m), tn=tn, col_scales=(qk_scales, None, sq_scales, None),
        transposed=(False, True, False, False), tk=tq, riders=riders, name="even_in_proj")
    a_out = _diff_attn(qk.reshape(batch, seq, -1), vt, lq1, lk1, lq2, lk2, sub_g, batch=batch,
                       seq=seq, tq=tq, lambda_init=lambda_init, name="diff_attn")
    b_out = _swa(sq.reshape(batch, seq, -1), skv.reshape(batch, seq, -1), sinks,
                 batch=batch, seq=seq, name="swa")
    w_o = w_out.astype(BF16)
    out = _out_proj([a_out.reshape(m, n_v), b_out.reshape(m, n_sq)],
                    [w_o[:n_v], w_o[n_v:]], xf, g_post, tm=min(512, m), sub=256,
                    name="even_out_proj")
    return out, cast


def _odd_mixer(xf, g_pre, w_in, conv_w, conv_b, gate_w, gate_b, lru_lambda, w_out, g_post,
               *, batch, seq, riders):
    m = batch * seq
    width = w_in.shape[1] // 2
    w_b = w_in.astype(BF16)
    yb, xb, *cast = _norm_proj(xf, g_pre, [w_b[:, :width], w_b[:, width:]],
                               [jax.nn.gelu, _identity], [BF16, F32],
                               tm=min(512, m), tn=512, riders=riders, name="odd_in_proj")
    hy = _rglru(xb, yb, conv_w, conv_b, (0.5 * gate_w).astype(BF16), 0.5 * gate_b, lru_lambda,
                batch=batch, seq=seq, chunk=min(512, seq), name="rglru")
    out = _out_proj([hy.reshape(m, width)], [w_out.astype(BF16)], xf, g_post,
                    tm=min(512, m), sub=256, name="odd_out_proj")
    return out, cast


def kernel(x, even_w_in, even_lam_q1, even_lam_k1, even_lam_q2, even_lam_k2, even_subln_g,
           even_sinks, even_w_out, odd_w_in, odd_conv_w, odd_conv_b, odd_gate_w, odd_gate_b,
           odd_lru_lambda, odd_w_out, pre_mix_g, post_mix_g, pre_mlp_g, post_mlp_g, mlp_w1,
           mlp_w2):
    batch, seq, d = x.shape
    m = batch * seq
    xf = x.reshape(m, d)
    depth = pre_mix_g.shape[0]
    for layer in range(depth):
        riders = ((mlp_w1, layer), (mlp_w2, layer))
        if layer % 2 == 0:
            e = layer // 2
            lambda_init = 0.8 - 0.6 * math.exp(-0.3 * layer)
            xf, (w1_b, w2_b) = _even_mixer(
                xf, pre_mix_g[layer], even_w_in[e], even_lam_q1[e], even_lam_k1[e],
                even_lam_q2[e], even_lam_k2[e], even_subln_g[e], even_sinks[e], even_w_out[e],
                post_mix_g[layer], batch=batch, seq=seq, lambda_init=lambda_init, riders=riders)
        else:
            o = layer // 2
            xf, (w1_b, w2_b) = _odd_mixer(
                xf, pre_mix_g[layer], odd_w_in[o], odd_conv_w[o], odd_conv_b[o], odd_gate_w[o],
                odd_gate_b[o], odd_lru_lambda[o], odd_w_out[o], post_mix_g[layer],
                batch=batch, seq=seq, riders=riders)
        xf = _mlp(xf, pre_mlp_g[layer], w1_b, w2_b, post_mlp_g[layer],
                  tm=min(512, m), tf=1024, sub=min(256, m), name=f"mlp_{layer}")
    return xf.reshape(batch, seq, d)
```

```python
import functools
import math

import jax
import jax.numpy as jnp
from jax import lax
from jax.experimental import pallas as pl
from jax.experimental.pallas import tpu as pltpu

F32 = jnp.float32
BF16 = jnp.bfloat16

EPS = 1e-6
HEAD_DIM = 64
DIFF_HEADS = 8
SWA_Q_HEADS = 16
SWA_KV_HEADS = 2
SWA_GROUP = SWA_Q_HEADS // SWA_KV_HEADS
SWA_BLOCK = 128
LRU_BLOCKS = 8
CONV_WIDTH = 4
LRU_C = 8.0

LANES = 128
SUBLANES = 8
BF16_SUBLANES = 16
MIB = 1024 * 1024

ROW_TILE = 512
COL_CHUNK = 512
FF_TILE = 1024
SUB_ROWS = 256
DIFF_TILE = 256
LRU_CHUNK = 512
VMEM_MIB = {"in_proj": 56, "diff_attn": 48, "swa": 40, "out_proj": 52, "mlp": 52, "rglru": 40}


def _params(semantics, vmem_mib):
    return pltpu.CompilerParams(dimension_semantics=semantics,
                                vmem_limit_bytes=vmem_mib * MIB)


def _rmsnorm(x, g):
    ms = jnp.mean(x * x, axis=-1, keepdims=True)
    return x * lax.rsqrt(ms + EPS) * g


def _cast_rider_specs(riders, n_steps):
    in_specs, out_specs, shapes = [], [], []
    for stack, layer in riders:
        _, rows, cols = stack.shape
        block_rows = rows // n_steps
        assert block_rows % BF16_SUBLANES == 0, "rider blocks must be whole bf16 row tiles"
        in_specs.append(pl.BlockSpec((None, block_rows, cols),
                                     functools.partial(lambda i, l: (l, i, 0), l=layer)))
        out_specs.append(pl.BlockSpec((block_rows, cols), lambda i: (i, 0)))
        shapes.append(jax.ShapeDtypeStruct((rows, cols), BF16))
    return in_specs, out_specs, shapes


def _cast_riders(in_refs, out_refs):
    for src_ref, dst_ref in zip(in_refs, out_refs):
        dst_ref[...] = src_ref[...].astype(dst_ref.dtype)


def _resident(shape):
    return pl.BlockSpec(shape, lambda *_: (0,) * len(shape), pipeline_mode=pl.Buffered(1))


def _norm_proj_kernel(x_ref, g_ref, *refs, epilogues, col_scales, transposed, tn, n_riders):
    n = len(epilogues)
    w_refs, rider_in = refs[:n], refs[n:n + n_riders]
    o_refs, rider_out = refs[n + n_riders:2 * n + n_riders], refs[2 * n + n_riders:]
    _cast_riders(rider_in, rider_out)
    x = x_ref[...]
    rs = lax.rsqrt(jnp.mean(x * x, axis=-1, keepdims=True) + EPS)
    xg = (x * g_ref[...]).astype(BF16)
    jobs = sorted((c0, k) for k in range(n) for c0 in range(0, w_refs[k].shape[1], tn))
    for c0, k in jobs:
        w_ref, o_ref, scales = w_refs[k], o_refs[k], col_scales[k]
        c1 = min(c0 + tn, w_ref.shape[1])
        row_scale = rs if scales is None else rs * scales[c0 // tn]
        res = epilogues[k](
            jnp.dot(xg, w_ref[:, c0:c1], preferred_element_type=F32) * row_scale)
        tk = transposed[k]
        if tk is not None:
            res_t = res.T.astype(o_ref.dtype)
            for c in range(o_ref.shape[0]):
                o_ref[c, c0:c1, :] = res_t[:, c * tk:(c + 1) * tk]
        else:
            o_ref[:, c0:c1] = res.astype(o_ref.dtype)


def _norm_proj(x, g, ws, epilogues, out_dtypes, *, tm, tn, name, col_scales=None,
               transposed=None, riders=()):
    m, d = x.shape
    rider_in_specs, rider_out_specs, rider_shapes = _cast_rider_specs(riders, m // tm)
    transposed = transposed or (None,) * len(ws)
    col_scales = col_scales or (None,) * len(ws)
    out_specs, out_shape = [], []
    for w, dt, tk in zip(ws, out_dtypes, transposed):
        n = w.shape[1]
        if tk is not None:
            out_specs.append(pl.BlockSpec((tm // tk, n, tk), lambda i: (i, 0, 0)))
            out_shape.append(jax.ShapeDtypeStruct((m // tk, n, tk), dt))
        else:
            out_specs.append(pl.BlockSpec((tm, n), lambda i: (i, 0)))
            out_shape.append(jax.ShapeDtypeStruct((m, n), dt))
    return pl.pallas_call(
        functools.partial(_norm_proj_kernel, epilogues=tuple(epilogues),
                          col_scales=tuple(col_scales), transposed=tuple(transposed),
                          tn=tn, n_riders=len(riders)),
        grid=(m // tm,),
        in_specs=[pl.BlockSpec((tm, d), lambda i: (i, 0)), _resident((1, d))]
                 + [_resident(w.shape) for w in ws] + rider_in_specs,
        out_specs=out_specs + rider_out_specs,
        out_shape=out_shape + rider_shapes,
        compiler_params=_params(("parallel",), VMEM_MIB["in_proj"]),
        name=name,
    )(x, g.reshape(1, d), *ws, *[stack for stack, _ in riders])


DIFF_HEADS_PER_STEP = 4
LOG2_E = 1.4426950408889634
QK_SCALE_LOG2 = HEAD_DIM ** -0.5 * LOG2_E
SUM_ROWS = 16


def _diff_attn_kernel(lq1_ref, lk1_ref, lq2_ref, lk2_ref, g_ref, q_ref, k_ref, vt_ref,
                      o_ref, *scratch, tq, lambda_init):
    hps = DIFF_HEADS_PER_STEP
    s_refs, qzt_refs, m_refs, acc_refs = (scratch[n * hps:(n + 1) * hps] for n in range(4))
    s_len = q_ref.shape[0]
    tk = tq
    heads = range(hps)
    last_head = hps - 1
    ones_rows = jnp.ones((SUM_ROWS, tk), BF16)
    lam = (jnp.exp(jnp.sum(lq1_ref[...] * lk1_ref[...], keepdims=True))
           - jnp.exp(jnp.sum(lq2_ref[...] * lk2_ref[...], keepdims=True)) + lambda_init)
    first_map = lax.broadcasted_iota(jnp.int32, (LANES, tq), 0) < HEAD_DIM
    kpos = lax.broadcasted_iota(jnp.int32, (tk, 2 * tq), 0)
    qcol = lax.broadcasted_iota(jnp.int32, (tk, 2 * tq), 1)
    causal = kpos <= jnp.where(qcol >= tq, qcol - tq, qcol)

    def scores(hh, j):
        cols = slice(hh * LANES, (hh + 1) * LANES)
        k = k_ref[pl.ds(pl.multiple_of(j * tk, tk), tk), cols]
        s_refs[hh][...] = jnp.dot(k, qzt_refs[hh][...], preferred_element_type=F32)

    def consume(hh, j, masked):
        cols = slice(hh * LANES, (hh + 1) * LANES)
        st = s_refs[hh][...]
        if masked:
            st = jnp.where(causal, st, -jnp.inf)
        m_old = m_refs[hh][...]
        m_new = jnp.maximum(m_old, jnp.max(st, axis=0, keepdims=True))
        alpha = jnp.exp2(m_old - m_new)
        p = jnp.exp2(st - m_new).astype(BF16)
        vt_ones = jnp.concatenate([vt_ref[j, cols, :], ones_rows], axis=0)
        acc_refs[hh][...] = alpha * acc_refs[hh][...] + jnp.dot(
            vt_ones, p, preferred_element_type=F32)
        m_refs[hh][...] = m_new

    def start_q_tile(i):
        r0 = pl.multiple_of(i * tq, tq)
        for hh in heads:
            qt = q_ref[pl.ds(r0, tq), hh * LANES:(hh + 1) * LANES].astype(F32).T
            qzt_refs[hh][:, 0:tq] = jnp.where(first_map, qt, 0.0).astype(BF16)
            qzt_refs[hh][:, tq:2 * tq] = jnp.where(first_map, 0.0, qt).astype(BF16)
            if hh != last_head:
                scores(hh, 0)

    def reset_state():
        for hh in heads:
            m_refs[hh][...] = jnp.full(m_refs[hh].shape, -jnp.inf, F32)
            acc_refs[hh][...] = jnp.zeros(acc_refs[hh].shape, F32)

    def step(j):
        scores(last_head, j)
        for hh in heads:
            consume(hh, j, False)
            if hh != last_head:
                scores(hh, j + 1)

    def finish_q_tile(i, odd):
        r0 = pl.multiple_of(i * tq, tq)
        if odd:
            step(i - 1)
        scores(last_head, i)
        for hh in heads:
            consume(hh, i, True)
        start_q_tile(jnp.minimum(i + 1, n_q - 1))
        for hh in heads:
            acc = acc_refs[hh][0:LANES, :]
            inv_l = 1.0 / acc_refs[hh][LANES:LANES + 1, :]
            ot = acc[:, :tq] * inv_l[:, :tq] - lam * (acc[:, tq:] * inv_l[:, tq:])
            o = _rmsnorm(ot.T, g_ref[...]) * (1.0 - lambda_init)
            o_ref[pl.ds(r0, tq), hh * LANES:(hh + 1) * LANES] = o.astype(o_ref.dtype)
        reset_state()

    def q_body(i, carry):
        def pair(jp, c):
            step(2 * jp)
            step(2 * jp + 1)
            return c

        lax.fori_loop(0, i // 2, pair, 0)
        pl.when(i % 2 == 1)(lambda: finish_q_tile(i, True))
        pl.when(i % 2 == 0)(lambda: finish_q_tile(i, False))
        return carry

    n_q = s_len // tq
    reset_state()
    start_q_tile(0)
    lax.fori_loop(0, n_q, q_body, 0)


def _diff_attn(proj, vt, lq1, lk1, lq2, lk2, sub_g, *, batch, seq, tq, lambda_init, name):
    hps = DIFF_HEADS_PER_STEP
    width = hps * LANES
    groups = DIFF_HEADS // hps
    vec = pl.BlockSpec((1, HEAD_DIM), lambda b, h: (0, 0))
    return pl.pallas_call(
        functools.partial(_diff_attn_kernel, tq=tq, lambda_init=lambda_init),
        grid=(batch, groups),
        in_specs=[vec, vec, vec, vec,
                  pl.BlockSpec((1, LANES), lambda b, h: (0, 0)),
                  pl.BlockSpec((None, seq, width), lambda b, h: (b, 0, h)),
                  pl.BlockSpec((None, seq, width), lambda b, h: (b, 0, groups + h)),
                  pl.BlockSpec((seq // tq, width, tq), lambda b, h: (b, h, 0))],
        out_specs=pl.BlockSpec((None, seq, width), lambda b, h: (b, 0, h)),
        out_shape=jax.ShapeDtypeStruct((batch, seq, DIFF_HEADS * LANES), BF16),
        scratch_shapes=[pltpu.VMEM((tq, 2 * tq), F32)] * hps
                       + [pltpu.VMEM((LANES, 2 * tq), BF16)] * hps
                       + [pltpu.VMEM((1, 2 * tq), F32)] * hps
                       + [pltpu.VMEM((LANES + SUM_ROWS, 2 * tq), F32)] * hps,
        compiler_params=_params(("parallel", "parallel"), VMEM_MIB["diff_attn"]),
        name=name,
    )(lq1.reshape(1, -1), lk1.reshape(1, -1), lq2.reshape(1, -1), lk2.reshape(1, -1),
      sub_g.reshape(1, -1), proj, proj, vt)


SWA_BLOCKS_PER_TRIP = 4


def _swa_kernel(sinks_ref, q_ref, kk_ref, vvt_ref, o_ref):
    s_len = q_ref.shape[0]
    blk = SWA_BLOCK
    nkeys = 2 * blk
    kvh = pl.program_id(1)
    first_head_rows = lax.broadcasted_iota(jnp.int32, (LANES, blk), 0) < HEAD_DIM
    first_head_lanes = lax.broadcasted_iota(jnp.int32, (blk, LANES), 1) < HEAD_DIM
    sinks = [sinks_ref[kvh * SWA_GROUP + g] * LOG2_E for g in range(SWA_GROUP)]
    query_minus_key = (lax.broadcasted_iota(jnp.int32, (nkeys, blk), 1)
                       - lax.broadcasted_iota(jnp.int32, (nkeys, blk), 0))

    def window(n):
        first = jnp.maximum(n - 1, 0)
        return first, pl.multiple_of(first * blk, blk)

    def block_scores(n):
        r0 = pl.multiple_of(n * blk, blk)
        _, k0 = window(n)
        qparts = []
        for gp in range(SWA_GROUP // 2):
            qt = q_ref[pl.ds(r0, blk), gp * LANES:(gp + 1) * LANES].astype(F32).T
            qparts.append(jnp.where(first_head_rows, qt, 0.0).astype(BF16))
            qparts.append(jnp.where(first_head_rows, 0.0, qt).astype(BF16))
        qzt = jnp.concatenate(qparts, axis=1)
        return jnp.dot(kk_ref[pl.ds(k0, nkeys), :], qzt, preferred_element_type=F32)

    def block_finish(n, st):
        r0 = pl.multiple_of(n * blk, blk)
        first, k0 = window(n)
        back = (r0 - k0) + query_minus_key
        valid = jnp.logical_and(back >= 0, back < blk)
        ps = []
        for g in range(SWA_GROUP):
            sg = jnp.where(valid, st[:, g * blk:(g + 1) * blk], -jnp.inf)
            m = jnp.maximum(jnp.max(sg, axis=0, keepdims=True), sinks[g])
            e = jnp.exp2(sg - m)
            denom = jnp.sum(e, axis=0, keepdims=True) + jnp.exp2(sinks[g] - m)
            ps.append((e * (1.0 / denom)).astype(BF16))
        p = jnp.concatenate(ps, axis=1)
        vvt = jnp.concatenate([vvt_ref[first], vvt_ref[first + 1]], axis=1)
        ot = jnp.dot(vvt, p, preferred_element_type=F32)
        for gp in range(SWA_GROUP // 2):
            lo = ot[:, (2 * gp) * blk:(2 * gp + 1) * blk].T
            hi = ot[:, (2 * gp + 1) * blk:(2 * gp + 2) * blk].T
            o_ref[pl.ds(r0, blk), gp * LANES:(gp + 1) * LANES] = jnp.where(
                first_head_lanes, lo, hi).astype(o_ref.dtype)

    def body(gi, c):
        blocks = [gi * SWA_BLOCKS_PER_TRIP + u for u in range(SWA_BLOCKS_PER_TRIP)]
        scores = [block_scores(n) for n in blocks]
        for n, st in zip(blocks, scores):
            block_finish(n, st)
        return c

    lax.fori_loop(0, s_len // (blk * SWA_BLOCKS_PER_TRIP), body, 0)


def _swa(q, kk, vvt, sinks, *, batch, seq, name):
    qw = SWA_GROUP * HEAD_DIM
    n_blocks = seq // SWA_BLOCK
    return pl.pallas_call(
        _swa_kernel,
        grid=(batch, SWA_KV_HEADS),
        in_specs=[pl.BlockSpec(memory_space=pltpu.SMEM),
                  pl.BlockSpec((None, seq, qw), lambda b, h: (b, 0, h)),
                  pl.BlockSpec((None, seq, LANES), lambda b, h: (b, 0, h)),
                  pl.BlockSpec((n_blocks, LANES, SWA_BLOCK), lambda b, h: (b, h, 0))],
        out_specs=pl.BlockSpec((None, seq, qw), lambda b, h: (b, 0, h)),
        out_shape=jax.ShapeDtypeStruct((batch, seq, SWA_Q_HEADS * HEAD_DIM), BF16),
        compiler_params=_params(("parallel", "parallel"), VMEM_MIB["swa"]),
        name=name,
    )(sinks, q, kk, vvt)


def _out_proj_kernel(*refs, n_in, sub):
    a_refs, w_refs = refs[:n_in], refs[n_in:2 * n_in]
    x_ref, g_ref, o_ref = refs[2 * n_in:]
    for r0 in range(0, x_ref.shape[0], sub):
        rows = slice(r0, r0 + sub)
        mix = jnp.dot(a_refs[0][rows, :], w_refs[0][...], preferred_element_type=F32)
        for a_ref, w_ref in zip(a_refs[1:], w_refs[1:]):
            mix = mix + jnp.dot(a_ref[rows, :], w_ref[...], preferred_element_type=F32)
        o_ref[rows, :] = x_ref[rows, :] + _rmsnorm(mix, g_ref[...])


def _out_proj(acts, ws, x, g, *, tm, sub, name):
    m, d = x.shape
    n_in = len(acts)
    return pl.pallas_call(
        functools.partial(_out_proj_kernel, n_in=n_in, sub=sub),
        grid=(m // tm,),
        in_specs=[pl.BlockSpec((tm, a.shape[1]), lambda i: (i, 0)) for a in acts]
                 + [_resident(w.shape) for w in ws]
                 + [pl.BlockSpec((tm, d), lambda i: (i, 0)), _resident((1, d))],
        out_specs=pl.BlockSpec((tm, d), lambda i: (i, 0)),
        out_shape=jax.ShapeDtypeStruct((m, d), F32),
        compiler_params=_params(("parallel",), VMEM_MIB["out_proj"]),
        name=name,
    )(*acts, *ws, x, g.reshape(1, d))


def _mlp_kernel(x_ref, g1_ref, w1_ref, w2_ref, g2_ref, o_ref, hn_ref, acc_ref, *, sub):
    j = pl.program_id(1)
    last = pl.num_programs(1) - 1
    tm = x_ref.shape[0]

    def step(first, final, rows_per_pass):
        for r0 in range(0, tm, rows_per_pass):
            rows = slice(r0, r0 + rows_per_pass)
            if first:
                hn = _rmsnorm(x_ref[rows, :], g1_ref[...]).astype(BF16)
                hn_ref[rows, :] = hn
            else:
                hn = hn_ref[rows, :]
            z = jnp.maximum(jnp.dot(hn, w1_ref[...], preferred_element_type=F32), 0.0)
            part = jnp.dot((z * z).astype(BF16), w2_ref[...], preferred_element_type=F32)
            acc = part if first else acc_ref[rows, :] + part
            if final:
                o_ref[rows, :] = x_ref[rows, :] + _rmsnorm(acc, g2_ref[...])
            else:
                acc_ref[rows, :] = acc

    pl.when(j == 0)(lambda: step(True, False, sub))
    pl.when(jnp.logical_and(j > 0, j < last))(lambda: step(False, False, tm))
    pl.when(j == last)(lambda: step(False, True, sub))


def _mlp(x, g1, w1, w2, g2, *, tm, tf, sub, name):
    m, d = x.shape
    f = w1.shape[1]
    assert f // tf >= 2, "the first and the last d_ff step must be different steps"
    return pl.pallas_call(
        functools.partial(_mlp_kernel, sub=sub),
        grid=(m // tm, f // tf),
        in_specs=[pl.BlockSpec((tm, d), lambda i, j: (i, 0)),
                  _resident((1, d)),
                  pl.BlockSpec((d, tf), lambda i, j: (0, j)),
                  pl.BlockSpec((tf, d), lambda i, j: (j, 0)),
                  _resident((1, d))],
        out_specs=pl.BlockSpec((tm, d), lambda i, j: (i, 0)),
        out_shape=jax.ShapeDtypeStruct((m, d), F32),
        scratch_shapes=[pltpu.VMEM((tm, d), BF16), pltpu.VMEM((tm, d), F32)],
        compiler_params=_params(("parallel", "arbitrary"), VMEM_MIB["mlp"]),
        name=name,
    )(x, g1.reshape(1, d), w1, w2, g2.reshape(1, d))


def _rglru_kernel(x_ref, y_ref, cw_ref, cb_ref, wr_ref, wi_ref, br_ref, bi_ref, lam_ref,
                  o_ref, xp_ref, *, chunk):
    s_len, c = x_ref.shape
    pad = SUBLANES
    xp_ref[0:pad, :] = jnp.zeros((pad, c), F32)
    xp_ref[pad:, :] = x_ref[...]
    log2_a_slope = jax.nn.softplus(-lam_ref[...]) * (-0.5 * LRU_C * LOG2_E)
    cw = cw_ref[...]
    groups = chunk // SUBLANES
    row_in_group = lax.broadcasted_iota(jnp.int32, (groups, SUBLANES, c), 1)

    def chunk_body(ci, h):
        t0 = ci * chunk
        own = xp_ref[t0 + pad:t0 + pad + chunk, :].reshape(groups, SUBLANES, c)
        prev = xp_ref[t0:t0 + chunk, :].reshape(groups, SUBLANES, c)
        xc = cb_ref[...] + cw[CONV_WIDTH - 1:CONV_WIDTH] * own
        for d in range(1, CONV_WIDTH):
            tap = cw[CONV_WIDTH - 1 - d:CONV_WIDTH - d]
            mixed = jnp.where(row_in_group >= SUBLANES - d, prev, own)
            xc = xc + tap * pltpu.roll(mixed, d, 1)
        xc = xc.reshape(chunk, c)
        xcb = xc.astype(BF16)
        tanh_r = jnp.tanh(jnp.dot(xcb, wr_ref[...], preferred_element_type=F32) + br_ref[...])
        tanh_i = jnp.tanh(jnp.dot(xcb, wi_ref[...], preferred_element_type=F32) + bi_ref[...])
        a = jnp.exp2(log2_a_slope * (tanh_r + 1.0))
        y = 1.0 - a * a
        gated_x = (0.5 * xc) * (tanh_i + 1.0)
        u = jnp.where(y > 0.0, y * lax.rsqrt(y), 0.0) * gated_x
        a = a.reshape(groups, SUBLANES, c)
        u = u.reshape(groups, SUBLANES, c)
        for d in (1, 2, 4):
            keep = row_in_group >= d
            a_prev = jnp.where(keep, pltpu.roll(a, d, 1), 1.0)
            u_prev = jnp.where(keep, pltpu.roll(u, d, 1), 0.0)
            u = a * u_prev + u
            a = a * a_prev
        states = []
        for g in range(groups):
            hv = a[g] * h + u[g]
            states.append(hv)
            h = jnp.broadcast_to(hv[SUBLANES - 1:SUBLANES, :], (SUBLANES, c))
        hs = jnp.concatenate(states, axis=0)
        o_ref[t0:t0 + chunk, :] = (
            hs * y_ref[t0:t0 + chunk, :].astype(F32)).astype(o_ref.dtype)
        return h

    h = jnp.zeros((SUBLANES, c), F32)
    for ci in range(s_len // chunk):
        h = chunk_body(ci, h)


def _rglru(xb, yb, conv_w, conv_b, gate_w, gate_b, lru_lambda, *, batch, seq, chunk, name):
    width = xb.shape[-1]
    c = width // LRU_BLOCKS
    row = pl.BlockSpec((1, c), lambda b, n: (0, n))
    gate = lambda g: pl.BlockSpec((None, None, c, c), lambda b, n: (g, n, 0, 0))
    gate_bias = lambda g: pl.BlockSpec((None, 1, c), lambda b, n: (g, 0, n))
    return pl.pallas_call(
        functools.partial(_rglru_kernel, chunk=chunk),
        grid=(batch, LRU_BLOCKS),
        in_specs=[pl.BlockSpec((None, seq, c), lambda b, n: (b, 0, n)),
                  pl.BlockSpec((None, seq, c), lambda b, n: (b, 0, n)),
                  pl.BlockSpec((CONV_WIDTH, c), lambda b, n: (0, n)),
                  row, gate(0), gate(1), gate_bias(0), gate_bias(1), row],
        out_specs=pl.BlockSpec((None, seq, c), lambda b, n: (b, 0, n)),
        out_shape=jax.ShapeDtypeStruct((batch, seq, width), BF16),
        scratch_shapes=[pltpu.VMEM((seq + SUBLANES, c), F32)],
        compiler_params=_params(("parallel", "parallel"), VMEM_MIB["rglru"]),
        name=name,
    )(xb.reshape(batch, seq, width), yb.reshape(batch, seq, width), conv_w,
      conv_b.reshape(1, width), gate_w, gate_w, gate_b.reshape(2, 1, width),
      gate_b.reshape(2, 1, width), lru_lambda.reshape(1, width))


def _identity(v):
    return v


def _even_mixer(xf, g_pre, w_in, lq1, lk1, lq2, lk2, sub_g, sinks, w_out, g_post,
                *, batch, seq, lambda_init, riders):
    m = batch * seq
    n_qk = 2 * DIFF_HEADS * HEAD_DIM
    n_v = DIFF_HEADS * 2 * HEAD_DIM
    n_sq = SWA_Q_HEADS * HEAD_DIM
    c0 = 2 * n_qk + n_v + n_sq
    ks = [w_in[:, c0 + h * HEAD_DIM:c0 + (h + 1) * HEAD_DIM] for h in range(SWA_KV_HEADS)]
    c1 = c0 + SWA_KV_HEADS * HEAD_DIM
    vs = [w_in[:, c1 + h * HEAD_DIM:c1 + (h + 1) * HEAD_DIM] for h in range(SWA_KV_HEADS)]
    w_qk = w_in[:, :2 * n_qk].astype(BF16)
    w_v = w_in[:, 2 * n_qk:2 * n_qk + n_v].astype(BF16)
    w_sq = w_in[:, 2 * n_qk + n_v:c0].astype(BF16)
    w_sk = jnp.concatenate([t for k in ks for t in (k, k)], axis=1).astype(BF16)
    w_sv = jnp.concatenate([t for v in vs for t in (v, v)], axis=1).astype(BF16)

    tq = min(DIFF_TILE, seq)
    tn = COL_CHUNK
    qk_scales = tuple(QK_SCALE_LOG2 if c < n_qk else 1.0 for c in range(0, 2 * n_qk, tn))
    sq_scales = (QK_SCALE_LOG2,) * (n_sq // tn)
    qk, vt, sq, sk, svt, *cast = _norm_proj(
        xf, g_pre, [w_qk, w_v, w_sq, w_sk, w_sv], [_identity] * 5, [BF16] * 5,
        tm=min(ROW_TILE, m), tn=tn, col_scales=(qk_scales, None, sq_scales, None, None),
        transposed=(None, tq, None, None, SWA_BLOCK), riders=riders, name="even_in_proj")
    a_out = _diff_attn(qk.reshape(batch, seq, -1), vt, lq1, lk1, lq2, lk2, sub_g, batch=batch,
                       seq=seq, tq=tq, lambda_init=lambda_init, name="diff_attn")
    b_out = _swa(sq.reshape(batch, seq, -1), sk.reshape(batch, seq, -1), svt, sinks,
                 batch=batch, seq=seq, name="swa")
    w_o = w_out.astype(BF16)
    out = _out_proj([a_out.reshape(m, n_v), b_out.reshape(m, n_sq)],
                    [w_o[:n_v], w_o[n_v:]], xf, g_post, tm=min(ROW_TILE, m),
                    sub=SUB_ROWS, name="even_out_proj")
    return out, cast


def _odd_mixer(xf, g_pre, w_in, conv_w, conv_b, gate_w, gate_b, lru_lambda, w_out, g_post,
               *, batch, seq, riders):
    m = batch * seq
    width = w_in.shape[1] // 2
    w_b = w_in.astype(BF16)
    yb, xb, *cast = _norm_proj(xf, g_pre, [w_b[:, :width], w_b[:, width:]],
                               [jax.nn.gelu, _identity], [BF16, F32],
                               tm=min(ROW_TILE, m), tn=COL_CHUNK, riders=riders,
                               name="odd_in_proj")
    hy = _rglru(xb, yb, conv_w, conv_b, (0.5 * gate_w).astype(BF16), 0.5 * gate_b, lru_lambda,
                batch=batch, seq=seq, chunk=min(LRU_CHUNK, seq), name="rglru")
    out = _out_proj([hy.reshape(m, width)], [w_out.astype(BF16)], xf, g_post,
                    tm=min(ROW_TILE, m), sub=SUB_ROWS, name="odd_out_proj")
    return out, cast


def kernel(x, even_w_in, even_lam_q1, even_lam_k1, even_lam_q2, even_lam_k2, even_subln_g,
           even_sinks, even_w_out, odd_w_in, odd_conv_w, odd_conv_b, odd_gate_w, odd_gate_b,
           odd_lru_lambda, odd_w_out, pre_mix_g, post_mix_g, pre_mlp_g, post_mlp_g, mlp_w1,
           mlp_w2):
    batch, seq, d = x.shape
    m = batch * seq
    xf = x.reshape(m, d)
    depth = pre_mix_g.shape[0]
    for layer in range(depth):
        riders = ((mlp_w1, layer), (mlp_w2, layer))
        if layer % 2 == 0:
            e = layer // 2
            lambda_init = 0.8 - 0.6 * math.exp(-0.3 * layer)
            xf, (w1_b, w2_b) = _even_mixer(
                xf, pre_mix_g[layer], even_w_in[e], even_lam_q1[e], even_lam_k1[e],
                even_lam_q2[e], even_lam_k2[e], even_subln_g[e], even_sinks[e], even_w_out[e],
                post_mix_g[layer], batch=batch, seq=seq, lambda_init=lambda_init, riders=riders)
        else:
            o = layer // 2
            xf, (w1_b, w2_b) = _odd_mixer(
                xf, pre_mix_g[layer], odd_w_in[o], odd_conv_w[o], odd_conv_b[o], odd_gate_w[o],
                odd_gate_b[o], odd_lru_lambda[o], odd_w_out[o], post_mix_g[layer],
                batch=batch, seq=seq, riders=riders)
        xf = _mlp(xf, pre_mlp_g[layer], w1_b, w2_b, post_mlp_g[layer],
                  tm=min(ROW_TILE, m), tf=FF_TILE, sub=min(SUB_ROWS, m), name=f"mlp_{layer}")
    return xf.reshape(batch, seq, d)
```

```python
import functools
import math

import jax
import jax.numpy as jnp
from jax import lax
from jax.experimental import pallas as pl
from jax.experimental.pallas import tpu as pltpu

F32 = jnp.float32
BF16 = jnp.bfloat16

EPS = 1e-6
HEAD_DIM = 64
DIFF_HEADS = 8
SWA_Q_HEADS = 16
SWA_KV_HEADS = 2
SWA_GROUP = SWA_Q_HEADS // SWA_KV_HEADS
SWA_BLOCK = 128
LRU_BLOCKS = 8
CONV_WIDTH = 4
LRU_C = 8.0

LANES = 128
SUBLANES = 8
BF16_SUBLANES = 16
MIB = 1024 * 1024

ROW_TILE = 512
COL_CHUNK = 512
FF_TILE = 1024
SUB_ROWS = 256
DIFF_TILE = 256
LRU_CHUNK = 512
VMEM_MIB = {"in_proj": 56, "diff_attn": 48, "swa": 40, "out_proj": 52, "mlp": 52, "rglru": 40}


def _params(semantics, vmem_mib):
    return pltpu.CompilerParams(dimension_semantics=semantics,
                                vmem_limit_bytes=vmem_mib * MIB)


def _rmsnorm(x, g):
    ms = jnp.mean(x * x, axis=-1, keepdims=True)
    return x * lax.rsqrt(ms + EPS) * g


def _cast_rider_specs(riders, n_steps):
    in_specs, out_specs, shapes = [], [], []
    for stack, layer in riders:
        _, rows, cols = stack.shape
        block_rows = rows // n_steps
        assert block_rows % BF16_SUBLANES == 0, "rider blocks must be whole bf16 row tiles"
        in_specs.append(pl.BlockSpec((None, block_rows, cols),
                                     functools.partial(lambda i, l: (l, i, 0), l=layer)))
        out_specs.append(pl.BlockSpec((block_rows, cols), lambda i: (i, 0)))
        shapes.append(jax.ShapeDtypeStruct((rows, cols), BF16))
    return in_specs, out_specs, shapes


def _cast_riders(in_refs, out_refs):
    for src_ref, dst_ref in zip(in_refs, out_refs):
        dst_ref[...] = src_ref[...].astype(dst_ref.dtype)


def _resident(shape):
    return pl.BlockSpec(shape, lambda *_: (0,) * len(shape), pipeline_mode=pl.Buffered(1))


def _norm_proj_kernel(x_ref, g_ref, *refs, epilogues, col_scales, transposed, tn, n_riders):
    n = len(epilogues)
    w_refs, rider_in = refs[:n], refs[n:n + n_riders]
    o_refs, rider_out = refs[n + n_riders:2 * n + n_riders], refs[2 * n + n_riders:]
    _cast_riders(rider_in, rider_out)
    x = x_ref[...]
    rs = lax.rsqrt(jnp.mean(x * x, axis=-1, keepdims=True) + EPS)
    xg = (x * g_ref[...]).astype(BF16)
    jobs = sorted((c0, k) for k in range(n) for c0 in range(0, w_refs[k].shape[1], tn))
    for c0, k in jobs:
        w_ref, o_ref, scales = w_refs[k], o_refs[k], col_scales[k]
        c1 = min(c0 + tn, w_ref.shape[1])
        row_scale = rs if scales is None else rs * scales[c0 // tn]
        res = epilogues[k](
            jnp.dot(xg, w_ref[:, c0:c1], preferred_element_type=F32) * row_scale)
        tk = transposed[k]
        if tk is not None:
            res_t = res.T.astype(o_ref.dtype)
            for c in range(o_ref.shape[0]):
                o_ref[c, c0:c1, :] = res_t[:, c * tk:(c + 1) * tk]
        else:
            o_ref[:, c0:c1] = res.astype(o_ref.dtype)


def _norm_proj(x, g, ws, epilogues, out_dtypes, *, tm, tn, name, col_scales=None,
               transposed=None, riders=()):
    m, d = x.shape
    rider_in_specs, rider_out_specs, rider_shapes = _cast_rider_specs(riders, m // tm)
    transposed = transposed or (None,) * len(ws)
    col_scales = col_scales or (None,) * len(ws)
    out_specs, out_shape = [], []
    for w, dt, tk in zip(ws, out_dtypes, transposed):
        n = w.shape[1]
        if tk is not None:
            out_specs.append(pl.BlockSpec((tm // tk, n, tk), lambda i: (i, 0, 0)))
            out_shape.append(jax.ShapeDtypeStruct((m // tk, n, tk), dt))
        else:
            out_specs.append(pl.BlockSpec((tm, n), lambda i: (i, 0)))
            out_shape.append(jax.ShapeDtypeStruct((m, n), dt))
    return pl.pallas_call(
        functools.partial(_norm_proj_kernel, epilogues=tuple(epilogues),
                          col_scales=tuple(col_scales), transposed=tuple(transposed),
                          tn=tn, n_riders=len(riders)),
        grid=(m // tm,),
        in_specs=[pl.BlockSpec((tm, d), lambda i: (i, 0)), _resident((1, d))]
                 + [_resident(w.shape) for w in ws] + rider_in_specs,
        out_specs=out_specs + rider_out_specs,
        out_shape=out_shape + rider_shapes,
        compiler_params=_params(("parallel",), VMEM_MIB["in_proj"]),
        name=name,
    )(x, g.reshape(1, d), *ws, *[stack for stack, _ in riders])


DIFF_HEADS_PER_STEP = 4
LOG2_E = 1.4426950408889634
QK_SCALE_LOG2 = HEAD_DIM ** -0.5 * LOG2_E
SUM_ROWS = 16


def _diff_attn_kernel(lq1_ref, lk1_ref, lq2_ref, lk2_ref, g_ref, q_ref, k_ref, vt_ref,
                      o_ref, *scratch, tq, lambda_init):
    hps = DIFF_HEADS_PER_STEP
    s_refs, qt_refs, m_refs, acc_refs = (scratch[n * hps:(n + 1) * hps] for n in range(4))
    s_len = q_ref.shape[0]
    tk = tq
    heads = range(hps)
    last_head = hps - 1
    ones_rows = jnp.ones((SUM_ROWS, tk), BF16)
    lam = (jnp.exp(jnp.sum(lq1_ref[...] * lk1_ref[...], keepdims=True))
           - jnp.exp(jnp.sum(lq2_ref[...] * lk2_ref[...], keepdims=True)) + lambda_init)
    kpos = lax.broadcasted_iota(jnp.int32, (tk, 2 * tq), 0)
    qcol = lax.broadcasted_iota(jnp.int32, (tk, 2 * tq), 1)
    causal = kpos <= jnp.where(qcol >= tq, qcol - tq, qcol)

    def scores(hh, j):
        cols = slice(hh * LANES, (hh + 1) * LANES)
        k = k_ref[pl.ds(pl.multiple_of(j * tk, tk), tk), cols]
        for c in range(2):
            dims = slice(c * HEAD_DIM, (c + 1) * HEAD_DIM)
            s_refs[hh][:, c * tq:(c + 1) * tq] = jnp.dot(
                k[:, dims], qt_refs[hh][dims, :], preferred_element_type=F32)

    def consume(hh, j, masked):
        cols = slice(hh * LANES, (hh + 1) * LANES)
        st = s_refs[hh][...]
        if masked:
            st = jnp.where(causal, st, -jnp.inf)
        m_old = m_refs[hh][...]
        m_new = jnp.maximum(m_old, jnp.max(st, axis=0, keepdims=True))
        alpha = jnp.exp2(m_old - m_new)
        p = jnp.exp2(st - m_new).astype(BF16)
        vt_ones = jnp.concatenate([vt_ref[j, cols, :], ones_rows], axis=0)
        acc_refs[hh][...] = alpha * acc_refs[hh][...] + jnp.dot(
            vt_ones, p, preferred_element_type=F32)
        m_refs[hh][...] = m_new

    def start_q_tile(i):
        r0 = pl.multiple_of(i * tq, tq)
        for hh in heads:
            qt = q_ref[pl.ds(r0, tq), hh * LANES:(hh + 1) * LANES].astype(F32).T
            qt_refs[hh][...] = qt.astype(BF16)
            if hh != last_head:
                scores(hh, 0)

    def reset_state():
        for hh in heads:
            m_refs[hh][...] = jnp.full(m_refs[hh].shape, -jnp.inf, F32)
            acc_refs[hh][...] = jnp.zeros(acc_refs[hh].shape, F32)

    def step(j):
        scores(last_head, j)
        for hh in heads:
            consume(hh, j, False)
            if hh != last_head:
                scores(hh, j + 1)

    def finish_q_tile(i, odd):
        r0 = pl.multiple_of(i * tq, tq)
        if odd:
            step(i - 1)
        scores(last_head, i)
        for hh in heads:
            consume(hh, i, True)
        start_q_tile(jnp.minimum(i + 1, n_q - 1))
        for hh in heads:
            acc = acc_refs[hh][0:LANES, :]
            inv_l = 1.0 / acc_refs[hh][LANES:LANES + 1, :]
            ot = acc[:, :tq] * inv_l[:, :tq] - lam * (acc[:, tq:] * inv_l[:, tq:])
            o = _rmsnorm(ot.T, g_ref[...]) * (1.0 - lambda_init)
            o_ref[pl.ds(r0, tq), hh * LANES:(hh + 1) * LANES] = o.astype(o_ref.dtype)
        reset_state()

    def q_body(i, carry):
        def pair(jp, c):
            step(2 * jp)
            step(2 * jp + 1)
            return c

        lax.fori_loop(0, i // 2, pair, 0)
        pl.when(i % 2 == 1)(lambda: finish_q_tile(i, True))
        pl.when(i % 2 == 0)(lambda: finish_q_tile(i, False))
        return carry

    n_q = s_len // tq
    reset_state()
    start_q_tile(0)
    lax.fori_loop(0, n_q, q_body, 0)


def _diff_attn(proj, vt, lq1, lk1, lq2, lk2, sub_g, *, batch, seq, tq, lambda_init, name):
    hps = DIFF_HEADS_PER_STEP
    width = hps * LANES
    groups = DIFF_HEADS // hps
    vec = pl.BlockSpec((1, HEAD_DIM), lambda b, h: (0, 0))
    return pl.pallas_call(
        functools.partial(_diff_attn_kernel, tq=tq, lambda_init=lambda_init),
        grid=(batch, groups),
        in_specs=[vec, vec, vec, vec,
                  pl.BlockSpec((1, LANES), lambda b, h: (0, 0)),
                  pl.BlockSpec((None, seq, width), lambda b, h: (b, 0, h)),
                  pl.BlockSpec((None, seq, width), lambda b, h: (b, 0, groups + h)),
                  pl.BlockSpec((seq // tq, width, tq), lambda b, h: (b, h, 0))],
        out_specs=pl.BlockSpec((None, seq, width), lambda b, h: (b, 0, h)),
        out_shape=jax.ShapeDtypeStruct((batch, seq, DIFF_HEADS * LANES), BF16),
        scratch_shapes=[pltpu.VMEM((tq, 2 * tq), F32)] * hps
                       + [pltpu.VMEM((LANES, tq), BF16)] * hps
                       + [pltpu.VMEM((1, 2 * tq), F32)] * hps
                       + [pltpu.VMEM((LANES + SUM_ROWS, 2 * tq), F32)] * hps,
        compiler_params=_params(("parallel", "parallel"), VMEM_MIB["diff_attn"]),
        name=name,
    )(lq1.reshape(1, -1), lk1.reshape(1, -1), lq2.reshape(1, -1), lk2.reshape(1, -1),
      sub_g.reshape(1, -1), proj, proj, vt)


SWA_BLOCKS_PER_TRIP = 4


def _swa_kernel(sinks_ref, q_ref, kk_ref, vvt_ref, o_ref):
    s_len = q_ref.shape[0]
    blk = SWA_BLOCK
    nkeys = 2 * blk
    kvh = pl.program_id(1)
    first_head_rows = lax.broadcasted_iota(jnp.int32, (LANES, blk), 0) < HEAD_DIM
    first_head_lanes = lax.broadcasted_iota(jnp.int32, (blk, LANES), 1) < HEAD_DIM
    sinks = [sinks_ref[kvh * SWA_GROUP + g] * LOG2_E for g in range(SWA_GROUP)]
    query_minus_key = (lax.broadcasted_iota(jnp.int32, (nkeys, blk), 1)
                       - lax.broadcasted_iota(jnp.int32, (nkeys, blk), 0))

    def window(n):
        first = jnp.maximum(n - 1, 0)
        return first, pl.multiple_of(first * blk, blk)

    def block_scores(n):
        r0 = pl.multiple_of(n * blk, blk)
        _, k0 = window(n)
        qparts = []
        for gp in range(SWA_GROUP // 2):
            qt = q_ref[pl.ds(r0, blk), gp * LANES:(gp + 1) * LANES].astype(F32).T
            qparts.append(jnp.where(first_head_rows, qt, 0.0).astype(BF16))
            qparts.append(jnp.where(first_head_rows, 0.0, qt).astype(BF16))
        qzt = jnp.concatenate(qparts, axis=1)
        return jnp.dot(kk_ref[pl.ds(k0, nkeys), :], qzt, preferred_element_type=F32)

    def block_finish(n, st):
        r0 = pl.multiple_of(n * blk, blk)
        first, k0 = window(n)
        back = (r0 - k0) + query_minus_key
        valid = jnp.logical_and(back >= 0, back < blk)
        ps = []
        for g in range(SWA_GROUP):
            sg = jnp.where(valid, st[:, g * blk:(g + 1) * blk], -jnp.inf)
            m = jnp.maximum(jnp.max(sg, axis=0, keepdims=True), sinks[g])
            e = jnp.exp2(sg - m)
            denom = jnp.sum(e, axis=0, keepdims=True) + jnp.exp2(sinks[g] - m)
            ps.append((e * (1.0 / denom)).astype(BF16))
        p = jnp.concatenate(ps, axis=1)
        vvt = jnp.concatenate([vvt_ref[first], vvt_ref[first + 1]], axis=1)
        ot = jnp.dot(vvt, p, preferred_element_type=F32)
        for gp in range(SWA_GROUP // 2):
            lo = ot[:, (2 * gp) * blk:(2 * gp + 1) * blk].T
            hi = ot[:, (2 * gp + 1) * blk:(2 * gp + 2) * blk].T
            o_ref[pl.ds(r0, blk), gp * LANES:(gp + 1) * LANES] = jnp.where(
                first_head_lanes, lo, hi).astype(o_ref.dtype)

    def body(gi, c):
        blocks = [gi * SWA_BLOCKS_PER_TRIP + u for u in range(SWA_BLOCKS_PER_TRIP)]
        scores = [block_scores(n) for n in blocks]
        for n, st in zip(blocks, scores):
            block_finish(n, st)
        return c

    lax.fori_loop(0, s_len // (blk * SWA_BLOCKS_PER_TRIP), body, 0)


def _swa(q, kk, vvt, sinks, *, batch, seq, name):
    qw = SWA_GROUP * HEAD_DIM
    n_blocks = seq // SWA_BLOCK
    return pl.pallas_call(
        _swa_kernel,
        grid=(batch, SWA_KV_HEADS),
        in_specs=[pl.BlockSpec(memory_space=pltpu.SMEM),
                  pl.BlockSpec((None, seq, qw), lambda b, h: (b, 0, h)),
                  pl.BlockSpec((None, seq, LANES), lambda b, h: (b, 0, h)),
                  pl.BlockSpec((n_blocks, LANES, SWA_BLOCK), lambda b, h: (b, h, 0))],
        out_specs=pl.BlockSpec((None, seq, qw), lambda b, h: (b, 0, h)),
        out_shape=jax.ShapeDtypeStruct((batch, seq, SWA_Q_HEADS * HEAD_DIM), BF16),
        compiler_params=_params(("parallel", "parallel"), VMEM_MIB["swa"]),
        name=name,
    )(sinks, q, kk, vvt)


def _out_proj_kernel(*refs, n_in, sub):
    a_refs = refs[:n_in]
    w_ref, x_ref, g_ref, o_ref = refs[n_in:]
    for r0 in range(0, x_ref.shape[0], sub):
        rows = slice(r0, r0 + sub)
        mix, k0 = None, 0
        for a_ref in a_refs:
            k1 = k0 + a_ref.shape[1]
            part = jnp.dot(a_ref[rows, :], w_ref[k0:k1, :], preferred_element_type=F32)
            mix = part if mix is None else mix + part
            k0 = k1
        o_ref[rows, :] = x_ref[rows, :] + _rmsnorm(mix, g_ref[...])


def _out_proj(acts, w, x, g, *, tm, sub, name):
    m, d = x.shape
    return pl.pallas_call(
        functools.partial(_out_proj_kernel, n_in=len(acts), sub=sub),
        grid=(m // tm,),
        in_specs=[pl.BlockSpec((tm, a.shape[1]), lambda i: (i, 0)) for a in acts]
                 + [_resident(w.shape),
                    pl.BlockSpec((tm, d), lambda i: (i, 0)), _resident((1, d))],
        out_specs=pl.BlockSpec((tm, d), lambda i: (i, 0)),
        out_shape=jax.ShapeDtypeStruct((m, d), F32),
        compiler_params=_params(("parallel",), VMEM_MIB["out_proj"]),
        name=name,
    )(*acts, w, x, g.reshape(1, d))


def _mlp_kernel(x_ref, g1_ref, w1_ref, w2_ref, g2_ref, o_ref, hn_ref, acc_ref, *, sub):
    j = pl.program_id(1)
    last = pl.num_programs(1) - 1
    tm = x_ref.shape[0]

    def step(first, final, rows_per_pass):
        for r0 in range(0, tm, rows_per_pass):
            rows = slice(r0, r0 + rows_per_pass)
            if first:
                hn = _rmsnorm(x_ref[rows, :], g1_ref[...]).astype(BF16)
                hn_ref[rows, :] = hn
            else:
                hn = hn_ref[rows, :]
            z = jnp.maximum(jnp.dot(hn, w1_ref[...], preferred_element_type=F32), 0.0)
            part = jnp.dot((z * z).astype(BF16), w2_ref[...], preferred_element_type=F32)
            acc = part if first else acc_ref[rows, :] + part
            if final:
                o_ref[rows, :] = x_ref[rows, :] + _rmsnorm(acc, g2_ref[...])
            else:
                acc_ref[rows, :] = acc

    pl.when(j == 0)(lambda: step(True, False, sub))
    pl.when(jnp.logical_and(j > 0, j < last))(lambda: step(False, False, tm))
    pl.when(j == last)(lambda: step(False, True, sub))


def _mlp(x, g1, w1, w2, g2, *, tm, tf, sub, name):
    m, d = x.shape
    f = w1.shape[1]
    assert f // tf >= 2, "the first and the last d_ff step must be different steps"
    return pl.pallas_call(
        functools.partial(_mlp_kernel, sub=sub),
        grid=(m // tm, f // tf),
        in_specs=[pl.BlockSpec((tm, d), lambda i, j: (i, 0)),
                  _resident((1, d)),
                  pl.BlockSpec((d, tf), lambda i, j: (0, j)),
                  pl.BlockSpec((tf, d), lambda i, j: (j, 0)),
                  _resident((1, d))],
        out_specs=pl.BlockSpec((tm, d), lambda i, j: (i, 0)),
        out_shape=jax.ShapeDtypeStruct((m, d), F32),
        scratch_shapes=[pltpu.VMEM((tm, d), BF16), pltpu.VMEM((tm, d), F32)],
        compiler_params=_params(("parallel", "arbitrary"), VMEM_MIB["mlp"]),
        name=name,
    )(x, g1.reshape(1, d), w1, w2, g2.reshape(1, d))


def _rglru_kernel(x_ref, y_ref, cw_ref, cb_ref, wr_ref, wi_ref, br_ref, bi_ref, lam_ref,
                  o_ref, xp_ref, *, chunk):
    s_len, c = x_ref.shape
    pad = SUBLANES
    xp_ref[0:pad, :] = jnp.zeros((pad, c), F32)
    xp_ref[pad:, :] = x_ref[...]
    log2_a_slope = jax.nn.softplus(-lam_ref[...]) * (-0.5 * LRU_C * LOG2_E)
    cw = cw_ref[...]
    groups = chunk // SUBLANES
    row_in_group = lax.broadcasted_iota(jnp.int32, (groups, SUBLANES, c), 1)

    def chunk_body(ci, h):
        t0 = ci * chunk
        own = xp_ref[t0 + pad:t0 + pad + chunk, :].reshape(groups, SUBLANES, c)
        prev = xp_ref[t0:t0 + chunk, :].reshape(groups, SUBLANES, c)
        xc = cb_ref[...] + cw[CONV_WIDTH - 1:CONV_WIDTH] * own
        for d in range(1, CONV_WIDTH):
            tap = cw[CONV_WIDTH - 1 - d:CONV_WIDTH - d]
            mixed = jnp.where(row_in_group >= SUBLANES - d, prev, own)
            xc = xc + tap * pltpu.roll(mixed, d, 1)
        xc = xc.reshape(chunk, c)
        xcb = xc.astype(BF16)
        tanh_r = jnp.tanh(jnp.dot(xcb, wr_ref[...], preferred_element_type=F32) + br_ref[...])
        tanh_i = jnp.tanh(jnp.dot(xcb, wi_ref[...], preferred_element_type=F32) + bi_ref[...])
        a = jnp.exp2(log2_a_slope * (tanh_r + 1.0))
        y = 1.0 - a * a
        gated_x = (0.5 * xc) * (tanh_i + 1.0)
        u = jnp.where(y > 0.0, y * lax.rsqrt(y), 0.0) * gated_x
        a = a.reshape(groups, SUBLANES, c)
        u = u.reshape(groups, SUBLANES, c)
        for d in (1, 2, 4):
            keep = row_in_group >= d
            a_prev = jnp.where(keep, pltpu.roll(a, d, 1), 1.0)
            u_prev = jnp.where(keep, pltpu.roll(u, d, 1), 0.0)
            u = a * u_prev + u
            a = a * a_prev
        states = []
        for g in range(groups):
            hv = a[g] * h + u[g]
            states.append(hv)
            h = jnp.broadcast_to(hv[SUBLANES - 1:SUBLANES, :], (SUBLANES, c))
        hs = jnp.concatenate(states, axis=0)
        o_ref[t0:t0 + chunk, :] = (
            hs * y_ref[t0:t0 + chunk, :].astype(F32)).astype(o_ref.dtype)
        return h

    h = jnp.zeros((SUBLANES, c), F32)
    for ci in range(s_len // chunk):
        h = chunk_body(ci, h)


def _rglru(xb, yb, conv_w, conv_b, gate_w, gate_b, lru_lambda, *, batch, seq, chunk, name):
    width = xb.shape[-1]
    c = width // LRU_BLOCKS
    row = pl.BlockSpec((1, c), lambda b, n: (0, n))
    gate = lambda g: pl.BlockSpec((None, None, c, c), lambda b, n: (g, n, 0, 0))
    gate_bias = lambda g: pl.BlockSpec((None, 1, c), lambda b, n: (g, 0, n))
    return pl.pallas_call(
        functools.partial(_rglru_kernel, chunk=chunk),
        grid=(batch, LRU_BLOCKS),
        in_specs=[pl.BlockSpec((None, seq, c), lambda b, n: (b, 0, n)),
                  pl.BlockSpec((None, seq, c), lambda b, n: (b, 0, n)),
                  pl.BlockSpec((CONV_WIDTH, c), lambda b, n: (0, n)),
                  row, gate(0), gate(1), gate_bias(0), gate_bias(1), row],
        out_specs=pl.BlockSpec((None, seq, c), lambda b, n: (b, 0, n)),
        out_shape=jax.ShapeDtypeStruct((batch, seq, width), BF16),
        scratch_shapes=[pltpu.VMEM((seq + SUBLANES, c), F32)],
        compiler_params=_params(("parallel", "parallel"), VMEM_MIB["rglru"]),
        name=name,
    )(xb.reshape(batch, seq, width), yb.reshape(batch, seq, width), conv_w,
      conv_b.reshape(1, width), gate_w, gate_w, gate_b.reshape(2, 1, width),
      gate_b.reshape(2, 1, width), lru_lambda.reshape(1, width))


def _identity(v):
    return v


def _even_mixer(xf, g_pre, w_in, lq1, lk1, lq2, lk2, sub_g, sinks, w_out, g_post,
                *, batch, seq, lambda_init, riders):
    m = batch * seq
    n_qk = 2 * DIFF_HEADS * HEAD_DIM
    n_v = DIFF_HEADS * 2 * HEAD_DIM
    n_sq = SWA_Q_HEADS * HEAD_DIM
    c0 = 2 * n_qk + n_v + n_sq
    ks = [w_in[:, c0 + h * HEAD_DIM:c0 + (h + 1) * HEAD_DIM] for h in range(SWA_KV_HEADS)]
    c1 = c0 + SWA_KV_HEADS * HEAD_DIM
    vs = [w_in[:, c1 + h * HEAD_DIM:c1 + (h + 1) * HEAD_DIM] for h in range(SWA_KV_HEADS)]
    w_qk = w_in[:, :2 * n_qk].astype(BF16)
    w_v = w_in[:, 2 * n_qk:2 * n_qk + n_v].astype(BF16)
    w_sq = w_in[:, 2 * n_qk + n_v:c0].astype(BF16)
    w_sk = jnp.concatenate([t for k in ks for t in (k, k)], axis=1).astype(BF16)
    w_sv = jnp.concatenate([t for v in vs for t in (v, v)], axis=1).astype(BF16)

    tq = min(DIFF_TILE, seq)
    tn = COL_CHUNK
    qk_scales = tuple(QK_SCALE_LOG2 if c < n_qk else 1.0 for c in range(0, 2 * n_qk, tn))
    sq_scales = (QK_SCALE_LOG2,) * (n_sq // tn)
    qk, vt, sq, sk, svt, *cast = _norm_proj(
        xf, g_pre, [w_qk, w_v, w_sq, w_sk, w_sv], [_identity] * 5, [BF16] * 5,
        tm=min(ROW_TILE, m), tn=tn, col_scales=(qk_scales, None, sq_scales, None, None),
        transposed=(None, tq, None, None, SWA_BLOCK), riders=riders, name="even_in_proj")
    a_out = _diff_attn(qk.reshape(batch, seq, -1), vt, lq1, lk1, lq2, lk2, sub_g, batch=batch,
                       seq=seq, tq=tq, lambda_init=lambda_init, name="diff_attn")
    b_out = _swa(sq.reshape(batch, seq, -1), sk.reshape(batch, seq, -1), svt, sinks,
                 batch=batch, seq=seq, name="swa")
    out = _out_proj([a_out.reshape(m, n_v), b_out.reshape(m, n_sq)], w_out.astype(BF16),
                    xf, g_post, tm=min(ROW_TILE, m), sub=SUB_ROWS, name="even_out_proj")
    return out, cast


def _odd_mixer(xf, g_pre, w_in, conv_w, conv_b, gate_w, gate_b, lru_lambda, w_out, g_post,
               *, batch, seq, riders):
    m = batch * seq
    width = w_in.shape[1] // 2
    w_b = w_in.astype(BF16)
    yb, xb, *cast = _norm_proj(xf, g_pre, [w_b[:, :width], w_b[:, width:]],
                               [jax.nn.gelu, _identity], [BF16, F32],
                               tm=min(ROW_TILE, m), tn=COL_CHUNK, riders=riders,
                               name="odd_in_proj")
    hy = _rglru(xb, yb, conv_w, conv_b, (0.5 * gate_w).astype(BF16), 0.5 * gate_b, lru_lambda,
                batch=batch, seq=seq, chunk=min(LRU_CHUNK, seq), name="rglru")
    out = _out_proj([hy.reshape(m, width)], w_out.astype(BF16), xf, g_post,
                    tm=min(ROW_TILE, m), sub=SUB_ROWS, name="odd_out_proj")
    return out, cast


def kernel(x, even_w_in, even_lam_q1, even_lam_k1, even_lam_q2, even_lam_k2, even_subln_g,
           even_sinks, even_w_out, odd_w_in, odd_conv_w, odd_conv_b, odd_gate_w, odd_gate_b,
           odd_lru_lambda, odd_w_out, pre_mix_g, post_mix_g, pre_mlp_g, post_mlp_g, mlp_w1,
           mlp_w2):
    batch, seq, d = x.shape
    m = batch * seq
    xf = x.reshape(m, d)
    depth = pre_mix_g.shape[0]
    for layer in range(depth):
        riders = ((mlp_w1, layer), (mlp_w2, layer))
        if layer % 2 == 0:
            e = layer // 2
            lambda_init = 0.8 - 0.6 * math.exp(-0.3 * layer)
            xf, (w1_b, w2_b) = _even_mixer(
                xf, pre_mix_g[layer], even_w_in[e], even_lam_q1[e], even_lam_k1[e],
                even_lam_q2[e], even_lam_k2[e], even_subln_g[e], even_sinks[e], even_w_out[e],
                post_mix_g[layer], batch=batch, seq=seq, lambda_init=lambda_init, riders=riders)
        else:
            o = layer // 2
            xf, (w1_b, w2_b) = _odd_mixer(
                xf, pre_mix_g[layer], odd_w_in[o], odd_conv_w[o], odd_conv_b[o], odd_gate_w[o],
                odd_gate_b[o], odd_lru_lambda[o], odd_w_out[o], post_mix_g[layer],
                batch=batch, seq=seq, riders=riders)
        xf = _mlp(xf, pre_mlp_g[layer], w1_b, w2_b, post_mlp_g[layer],
                  tm=min(ROW_TILE, m), tf=FF_TILE, sub=min(SUB_ROWS, m), name=f"mlp_{layer}")
    return xf.reshape(batch, seq, d)
```

```python
import functools
import math

import jax
import jax.numpy as jnp
from jax import lax
from jax.experimental import pallas as pl
from jax.experimental.pallas import tpu as pltpu

F32 = jnp.float32
BF16 = jnp.bfloat16

EPS = 1e-6
HEAD_DIM = 64
DIFF_HEADS = 8
SWA_Q_HEADS = 16
SWA_KV_HEADS = 2
SWA_GROUP = SWA_Q_HEADS // SWA_KV_HEADS
SWA_BLOCK = 128
LRU_BLOCKS = 8
CONV_WIDTH = 4
LRU_C = 8.0

LANES = 128
SUBLANES = 8
BF16_SUBLANES = 16
MIB = 1024 * 1024

ROW_TILE = 512
COL_CHUNK = 512
FF_TILE = 1024
SUB_ROWS = 256
DIFF_TILE = 256
LRU_CHUNK = 512
VMEM_MIB = {"in_proj": 56, "diff_attn": 48, "swa": 40, "out_proj": 52, "mlp": 52, "rglru": 40}


def _params(semantics, vmem_mib):
    return pltpu.CompilerParams(dimension_semantics=semantics,
                                vmem_limit_bytes=vmem_mib * MIB)


def _rmsnorm(x, g):
    ms = jnp.mean(x * x, axis=-1, keepdims=True)
    return x * lax.rsqrt(ms + EPS) * g


def _cast_rider_specs(riders, n_steps):
    in_specs, out_specs, shapes = [], [], []
    for stack, layer in riders:
        _, rows, cols = stack.shape
        block_rows = rows // n_steps
        assert block_rows % BF16_SUBLANES == 0, "rider blocks must be whole bf16 row tiles"
        in_specs.append(pl.BlockSpec((None, block_rows, cols),
                                     functools.partial(lambda i, l: (l, i, 0), l=layer)))
        out_specs.append(pl.BlockSpec((block_rows, cols), lambda i: (i, 0)))
        shapes.append(jax.ShapeDtypeStruct((rows, cols), BF16))
    return in_specs, out_specs, shapes


def _cast_riders(in_refs, out_refs):
    for src_ref, dst_ref in zip(in_refs, out_refs):
        dst_ref[...] = src_ref[...].astype(dst_ref.dtype)


def _resident(shape):
    return pl.BlockSpec(shape, lambda *_: (0,) * len(shape), pipeline_mode=pl.Buffered(1))


def _norm_proj_kernel(x_ref, g_ref, *refs, epilogues, col_scales, transposed, tn, n_riders):
    n = len(epilogues)
    w_refs, rider_in = refs[:n], refs[n:n + n_riders]
    o_refs, rider_out = refs[n + n_riders:2 * n + n_riders], refs[2 * n + n_riders:]
    _cast_riders(rider_in, rider_out)
    x = x_ref[...]
    rs = lax.rsqrt(jnp.mean(x * x, axis=-1, keepdims=True) + EPS)
    xg = (x * g_ref[...]).astype(BF16)
    jobs = sorted((c0, k) for k in range(n) for c0 in range(0, w_refs[k].shape[1], tn))
    for c0, k in jobs:
        w_ref, o_ref, scales = w_refs[k], o_refs[k], col_scales[k]
        c1 = min(c0 + tn, w_ref.shape[1])
        row_scale = rs if scales is None else rs * scales[c0 // tn]
        res = epilogues[k](
            jnp.dot(xg, w_ref[:, c0:c1], preferred_element_type=F32) * row_scale)
        tk = transposed[k]
        if tk is not None:
            res_t = res.T.astype(o_ref.dtype)
            for c in range(o_ref.shape[0]):
                o_ref[c, c0:c1, :] = res_t[:, c * tk:(c + 1) * tk]
        else:
            o_ref[:, c0:c1] = res.astype(o_ref.dtype)


def _norm_proj(x, g, ws, epilogues, out_dtypes, *, tm, tn, name, col_scales=None,
               transposed=None, riders=()):
    m, d = x.shape
    rider_in_specs, rider_out_specs, rider_shapes = _cast_rider_specs(riders, m // tm)
    transposed = transposed or (None,) * len(ws)
    col_scales = col_scales or (None,) * len(ws)
    out_specs, out_shape = [], []
    for w, dt, tk in zip(ws, out_dtypes, transposed):
        n = w.shape[1]
        if tk is not None:
            out_specs.append(pl.BlockSpec((tm // tk, n, tk), lambda i: (i, 0, 0)))
            out_shape.append(jax.ShapeDtypeStruct((m // tk, n, tk), dt))
        else:
            out_specs.append(pl.BlockSpec((tm, n), lambda i: (i, 0)))
            out_shape.append(jax.ShapeDtypeStruct((m, n), dt))
    return pl.pallas_call(
        functools.partial(_norm_proj_kernel, epilogues=tuple(epilogues),
                          col_scales=tuple(col_scales), transposed=tuple(transposed),
                          tn=tn, n_riders=len(riders)),
        grid=(m // tm,),
        in_specs=[pl.BlockSpec((tm, d), lambda i: (i, 0)), _resident((1, d))]
                 + [_resident(w.shape) for w in ws] + rider_in_specs,
        out_specs=out_specs + rider_out_specs,
        out_shape=out_shape + rider_shapes,
        compiler_params=_params(("parallel",), VMEM_MIB["in_proj"]),
        name=name,
    )(x, g.reshape(1, d), *ws, *[stack for stack, _ in riders])


DIFF_HEADS_PER_STEP = 4
LOG2_E = 1.4426950408889634
QK_SCALE_LOG2 = HEAD_DIM ** -0.5 * LOG2_E
SUM_ROWS = 16


def _diff_attn_kernel(lq1_ref, lk1_ref, lq2_ref, lk2_ref, g_ref, q_ref, k_ref, vt_ref,
                      o_ref, *scratch, tq, lambda_init):
    hps = DIFF_HEADS_PER_STEP
    s_refs, qt_refs, m_refs, acc_refs = (scratch[n * hps:(n + 1) * hps] for n in range(4))
    s_len = q_ref.shape[0]
    tk = tq
    heads = range(hps)
    last_head = hps - 1
    ones_rows = jnp.ones((SUM_ROWS, tk), BF16)
    lam = (jnp.exp(jnp.sum(lq1_ref[...] * lk1_ref[...], keepdims=True))
           - jnp.exp(jnp.sum(lq2_ref[...] * lk2_ref[...], keepdims=True)) + lambda_init)
    kpos = lax.broadcasted_iota(jnp.int32, (tk, 2 * tq), 0)
    qcol = lax.broadcasted_iota(jnp.int32, (tk, 2 * tq), 1)
    causal = kpos <= jnp.where(qcol >= tq, qcol - tq, qcol)

    def scores(hh, j):
        cols = slice(hh * LANES, (hh + 1) * LANES)
        k = k_ref[pl.ds(pl.multiple_of(j * tk, tk), tk), cols]
        for c in range(2):
            dims = slice(c * HEAD_DIM, (c + 1) * HEAD_DIM)
            s_refs[hh][:, c * tq:(c + 1) * tq] = jnp.dot(
                k[:, dims], qt_refs[hh][dims, :], preferred_element_type=F32)

    def consume(hh, j, masked):
        cols = slice(hh * LANES, (hh + 1) * LANES)
        st = s_refs[hh][...]
        if masked:
            st = jnp.where(causal, st, -jnp.inf)
        m_old = m_refs[hh][...]
        m_new = jnp.maximum(m_old, jnp.max(st, axis=0, keepdims=True))
        alpha = jnp.exp2(m_old - m_new)
        p = jnp.exp2(st - m_new).astype(BF16)
        vt_ones = jnp.concatenate([vt_ref[j, cols, :], ones_rows], axis=0)
        acc_refs[hh][...] = alpha * acc_refs[hh][...] + jnp.dot(
            vt_ones, p, preferred_element_type=F32)
        m_refs[hh][...] = m_new

    def start_q_tile(i):
        r0 = pl.multiple_of(i * tq, tq)
        for hh in heads:
            qt = q_ref[pl.ds(r0, tq), hh * LANES:(hh + 1) * LANES].astype(F32).T
            qt_refs[hh][...] = qt.astype(BF16)
            if hh != last_head:
                scores(hh, 0)

    def reset_state():
        for hh in heads:
            m_refs[hh][...] = jnp.full(m_refs[hh].shape, -jnp.inf, F32)
            acc_refs[hh][...] = jnp.zeros(acc_refs[hh].shape, F32)

    def step(j):
        scores(last_head, j)
        for hh in heads:
            consume(hh, j, False)
            if hh != last_head:
                scores(hh, j + 1)

    def finish_q_tile(i, odd):
        r0 = pl.multiple_of(i * tq, tq)
        if odd:
            step(i - 1)
        scores(last_head, i)
        for hh in heads:
            consume(hh, i, True)
        start_q_tile(jnp.minimum(i + 1, n_q - 1))
        for hh in heads:
            acc = acc_refs[hh][0:LANES, :]
            inv_l = 1.0 / acc_refs[hh][LANES:LANES + 1, :]
            ot = acc[:, :tq] * inv_l[:, :tq] - lam * (acc[:, tq:] * inv_l[:, tq:])
            o = _rmsnorm(ot.T, g_ref[...]) * (1.0 - lambda_init)
            o_ref[pl.ds(r0, tq), hh * LANES:(hh + 1) * LANES] = o.astype(o_ref.dtype)
        reset_state()

    def q_body(i, carry):
        def pair(jp, c):
            step(2 * jp)
            step(2 * jp + 1)
            return c

        lax.fori_loop(0, i // 2, pair, 0)
        pl.when(i % 2 == 1)(lambda: finish_q_tile(i, True))
        pl.when(i % 2 == 0)(lambda: finish_q_tile(i, False))
        return carry

    n_q = s_len // tq
    reset_state()
    start_q_tile(0)
    lax.fori_loop(0, n_q, q_body, 0)


def _diff_attn(proj, vt, lq1, lk1, lq2, lk2, sub_g, *, batch, seq, tq, lambda_init, name):
    hps = DIFF_HEADS_PER_STEP
    width = hps * LANES
    groups = DIFF_HEADS // hps
    vec = pl.BlockSpec((1, HEAD_DIM), lambda b, h: (0, 0))
    return pl.pallas_call(
        functools.partial(_diff_attn_kernel, tq=tq, lambda_init=lambda_init),
        grid=(batch, groups),
        in_specs=[vec, vec, vec, vec,
                  pl.BlockSpec((1, LANES), lambda b, h: (0, 0)),
                  pl.BlockSpec((None, seq, width), lambda b, h: (b, 0, h)),
                  pl.BlockSpec((None, seq, width), lambda b, h: (b, 0, groups + h)),
                  pl.BlockSpec((seq // tq, width, tq), lambda b, h: (b, h, 0))],
        out_specs=pl.BlockSpec((None, seq, width), lambda b, h: (b, 0, h)),
        out_shape=jax.ShapeDtypeStruct((batch, seq, DIFF_HEADS * LANES), BF16),
        scratch_shapes=[pltpu.VMEM((tq, 2 * tq), F32)] * hps
                       + [pltpu.VMEM((LANES, tq), BF16)] * hps
                       + [pltpu.VMEM((1, 2 * tq), F32)] * hps
                       + [pltpu.VMEM((LANES + SUM_ROWS, 2 * tq), F32)] * hps,
        compiler_params=_params(("parallel", "parallel"), VMEM_MIB["diff_attn"]),
        name=name,
    )(lq1.reshape(1, -1), lk1.reshape(1, -1), lq2.reshape(1, -1), lk2.reshape(1, -1),
      sub_g.reshape(1, -1), proj, proj, vt)


SWA_BLOCKS_PER_TRIP = 4


def _swa_kernel(sinks_ref, q_ref, kk_ref, vvt_ref, o_ref):
    s_len = q_ref.shape[0]
    blk = SWA_BLOCK
    nkeys = 2 * blk
    kvh = pl.program_id(1)
    first_head_rows = lax.broadcasted_iota(jnp.int32, (LANES, blk), 0) < HEAD_DIM
    first_head_lanes = lax.broadcasted_iota(jnp.int32, (blk, LANES), 1) < HEAD_DIM
    sinks = [sinks_ref[kvh * SWA_GROUP + g] * LOG2_E for g in range(SWA_GROUP)]
    query_minus_key = (lax.broadcasted_iota(jnp.int32, (nkeys, blk), 1)
                       - lax.broadcasted_iota(jnp.int32, (nkeys, blk), 0))

    def window(n):
        first = jnp.maximum(n - 1, 0)
        return first, pl.multiple_of(first * blk, blk)

    def block_scores(n):
        r0 = pl.multiple_of(n * blk, blk)
        _, k0 = window(n)
        qparts = []
        for gp in range(SWA_GROUP // 2):
            qt = q_ref[pl.ds(r0, blk), gp * LANES:(gp + 1) * LANES].astype(F32).T
            qparts.append(jnp.where(first_head_rows, qt, 0.0).astype(BF16))
            qparts.append(jnp.where(first_head_rows, 0.0, qt).astype(BF16))
        qzt = jnp.concatenate(qparts, axis=1)
        return jnp.dot(kk_ref[pl.ds(k0, nkeys), :], qzt, preferred_element_type=F32)

    def block_finish(n, st):
        r0 = pl.multiple_of(n * blk, blk)
        first, k0 = window(n)
        back = (r0 - k0) + query_minus_key
        valid = jnp.logical_and(back >= 0, back < blk)
        ps = []
        for g in range(SWA_GROUP):
            sg = jnp.where(valid, st[:, g * blk:(g + 1) * blk], -jnp.inf)
            m = jnp.maximum(jnp.max(sg, axis=0, keepdims=True), sinks[g])
            e = jnp.exp2(sg - m)
            denom = jnp.sum(e, axis=0, keepdims=True) + jnp.exp2(sinks[g] - m)
            ps.append((e * (1.0 / denom)).astype(BF16))
        p = jnp.concatenate(ps, axis=1)
        vvt = jnp.concatenate([vvt_ref[first], vvt_ref[first + 1]], axis=1)
        ot = jnp.dot(vvt, p, preferred_element_type=F32)
        for gp in range(SWA_GROUP // 2):
            lo = ot[:, (2 * gp) * blk:(2 * gp + 1) * blk].T
            hi = ot[:, (2 * gp + 1) * blk:(2 * gp + 2) * blk].T
            o_ref[pl.ds(r0, blk), gp * LANES:(gp + 1) * LANES] = jnp.where(
                first_head_lanes, lo, hi).astype(o_ref.dtype)

    def body(gi, c):
        blocks = [gi * SWA_BLOCKS_PER_TRIP + u for u in range(SWA_BLOCKS_PER_TRIP)]
        scores = [block_scores(n) for n in blocks]
        for n, st in zip(blocks, scores):
            block_finish(n, st)
        return c

    lax.fori_loop(0, s_len // (blk * SWA_BLOCKS_PER_TRIP), body, 0)


def _swa(q, kk, vvt, sinks, *, batch, seq, name):
    qw = SWA_GROUP * HEAD_DIM
    n_blocks = seq // SWA_BLOCK
    return pl.pallas_call(
        _swa_kernel,
        grid=(batch, SWA_KV_HEADS),
        in_specs=[pl.BlockSpec(memory_space=pltpu.SMEM),
                  pl.BlockSpec((None, seq, qw), lambda b, h: (b, 0, h)),
                  pl.BlockSpec((None, seq, LANES), lambda b, h: (b, 0, h)),
                  pl.BlockSpec((n_blocks, LANES, SWA_BLOCK), lambda b, h: (b, h, 0))],
        out_specs=pl.BlockSpec((None, seq, qw), lambda b, h: (b, 0, h)),
        out_shape=jax.ShapeDtypeStruct((batch, seq, SWA_Q_HEADS * HEAD_DIM), BF16),
        compiler_params=_params(("parallel", "parallel"), VMEM_MIB["swa"]),
        name=name,
    )(sinks, q, kk, vvt)


def _out_proj_kernel(*refs, n_in, sub):
    a_refs, w_refs = refs[:n_in], refs[n_in:2 * n_in]
    x_ref, g_ref, o_ref = refs[2 * n_in:]
    for r0 in range(0, x_ref.shape[0], sub):
        rows = slice(r0, r0 + sub)
        mix = jnp.dot(a_refs[0][rows, :], w_refs[0][...], preferred_element_type=F32)
        for a_ref, w_ref in zip(a_refs[1:], w_refs[1:]):
            mix = mix + jnp.dot(a_ref[rows, :], w_ref[...], preferred_element_type=F32)
        o_ref[rows, :] = x_ref[rows, :] + _rmsnorm(mix, g_ref[...])


def _out_proj(acts, ws, x, g, *, tm, sub, name):
    m, d = x.shape
    n_in = len(acts)
    return pl.pallas_call(
        functools.partial(_out_proj_kernel, n_in=n_in, sub=sub),
        grid=(m // tm,),
        in_specs=[pl.BlockSpec((tm, a.shape[1]), lambda i: (i, 0)) for a in acts]
                 + [_resident(w.shape) for w in ws]
                 + [pl.BlockSpec((tm, d), lambda i: (i, 0)), _resident((1, d))],
        out_specs=pl.BlockSpec((tm, d), lambda i: (i, 0)),
        out_shape=jax.ShapeDtypeStruct((m, d), F32),
        compiler_params=_params(("parallel",), VMEM_MIB["out_proj"]),
        name=name,
    )(*acts, *ws, x, g.reshape(1, d))


MLP_WEIGHT_BUFFERS = 3


def _mlp_kernel(x_ref, g1_ref, w1_hbm, w2_hbm, g2_ref, o_ref, hn_ref, acc_ref, w1_buf, w2_buf,
                sems, *, sub):
    j = pl.program_id(1)
    n_j = pl.num_programs(1)
    last = n_j - 1
    tm = x_ref.shape[0]
    tf = w1_buf.shape[2]
    depth = MLP_WEIGHT_BUFFERS
    step_id = pl.program_id(0) * n_j + j
    n_steps = pl.num_programs(0) * n_j

    def weight_copies(s):
        slot = s % depth
        f0 = pl.multiple_of((s % n_j) * tf, tf)
        return (pltpu.make_async_copy(w1_hbm.at[:, pl.ds(f0, tf)], w1_buf.at[slot],
                                      sems.at[0, slot]),
                pltpu.make_async_copy(w2_hbm.at[pl.ds(f0, tf), :], w2_buf.at[slot],
                                      sems.at[1, slot]))

    @pl.when(step_id == 0)
    def _():
        for s in range(depth - 1):
            for copy in weight_copies(s):
                copy.start()

    @pl.when(step_id + depth - 1 < n_steps)
    def _():
        for copy in weight_copies(step_id + depth - 1):
            copy.start()

    for copy in weight_copies(step_id):
        copy.wait()
    w1_ref = w1_buf.at[step_id % depth]
    w2_ref = w2_buf.at[step_id % depth]

    def step(first, final, rows_per_pass):
        for r0 in range(0, tm, rows_per_pass):
            rows = slice(r0, r0 + rows_per_pass)
            if first:
                hn = _rmsnorm(x_ref[rows, :], g1_ref[...]).astype(BF16)
                hn_ref[rows, :] = hn
            else:
                hn = hn_ref[rows, :]
            z = jnp.maximum(jnp.dot(hn, w1_ref[...], preferred_element_type=F32), 0.0)
            part = jnp.dot((z * z).astype(BF16), w2_ref[...], preferred_element_type=F32)
            acc = part if first else acc_ref[rows, :] + part
            if final:
                o_ref[rows, :] = x_ref[rows, :] + _rmsnorm(acc, g2_ref[...])
            else:
                acc_ref[rows, :] = acc

    pl.when(j == 0)(lambda: step(True, False, sub))
    pl.when(jnp.logical_and(j > 0, j < last))(lambda: step(False, False, tm))
    pl.when(j == last)(lambda: step(False, True, sub))


def _mlp(x, g1, w1, w2, g2, *, tm, tf, sub, name):
    m, d = x.shape
    f = w1.shape[1]
    assert f // tf >= 2, "the first and the last d_ff step must be different steps"
    assert (m // tm) * (f // tf) >= MLP_WEIGHT_BUFFERS - 1
    return pl.pallas_call(
        functools.partial(_mlp_kernel, sub=sub),
        grid=(m // tm, f // tf),
        in_specs=[pl.BlockSpec((tm, d), lambda i, j: (i, 0)),
                  _resident((1, d)),
                  pl.BlockSpec(memory_space=pl.ANY),
                  pl.BlockSpec(memory_space=pl.ANY),
                  _resident((1, d))],
        out_specs=pl.BlockSpec((tm, d), lambda i, j: (i, 0)),
        out_shape=jax.ShapeDtypeStruct((m, d), F32),
        scratch_shapes=[pltpu.VMEM((tm, d), BF16), pltpu.VMEM((tm, d), F32),
                        pltpu.VMEM((MLP_WEIGHT_BUFFERS, d, tf), BF16),
                        pltpu.VMEM((MLP_WEIGHT_BUFFERS, tf, d), BF16),
                        pltpu.SemaphoreType.DMA((2, MLP_WEIGHT_BUFFERS))],
        compiler_params=_params(("arbitrary", "arbitrary"), VMEM_MIB["mlp"]),
        name=name,
    )(x, g1.reshape(1, d), w1, w2, g2.reshape(1, d))


def _rglru_kernel(x_ref, y_ref, cw_ref, cb_ref, wr_ref, wi_ref, br_ref, bi_ref, lam_ref,
                  o_ref, xp_ref, *, chunk):
    s_len, c = x_ref.shape
    pad = SUBLANES
    xp_ref[0:pad, :] = jnp.zeros((pad, c), F32)
    xp_ref[pad:, :] = x_ref[...]
    log2_a_slope = jax.nn.softplus(-lam_ref[...]) * (-0.5 * LRU_C * LOG2_E)
    cw = cw_ref[...]
    groups = chunk // SUBLANES
    row_in_group = lax.broadcasted_iota(jnp.int32, (groups, SUBLANES, c), 1)

    def chunk_body(ci, h):
        t0 = ci * chunk
        own = xp_ref[t0 + pad:t0 + pad + chunk, :].reshape(groups, SUBLANES, c)
        prev = xp_ref[t0:t0 + chunk, :].reshape(groups, SUBLANES, c)
        xc = cb_ref[...] + cw[CONV_WIDTH - 1:CONV_WIDTH] * own
        for d in range(1, CONV_WIDTH):
            tap = cw[CONV_WIDTH - 1 - d:CONV_WIDTH - d]
            mixed = jnp.where(row_in_group >= SUBLANES - d, prev, own)
            xc = xc + tap * pltpu.roll(mixed, d, 1)
        xc = xc.reshape(chunk, c)
        xcb = xc.astype(BF16)
        tanh_r = jnp.tanh(jnp.dot(xcb, wr_ref[...], preferred_element_type=F32) + br_ref[...])
        tanh_i = jnp.tanh(jnp.dot(xcb, wi_ref[...], preferred_element_type=F32) + bi_ref[...])
        a = jnp.exp2(log2_a_slope * (tanh_r + 1.0))
        y = 1.0 - a * a
        gated_x = (0.5 * xc) * (tanh_i + 1.0)
        u = jnp.where(y > 0.0, y * lax.rsqrt(y), 0.0) * gated_x
        a = a.reshape(groups, SUBLANES, c)
        u = u.reshape(groups, SUBLANES, c)
        for d in (1, 2, 4):
            keep = row_in_group >= d
            a_prev = jnp.where(keep, pltpu.roll(a, d, 1), 1.0)
            u_prev = jnp.where(keep, pltpu.roll(u, d, 1), 0.0)
            u = a * u_prev + u
            a = a * a_prev
        states = []
        for g in range(groups):
            hv = a[g] * h + u[g]
            states.append(hv)
            h = jnp.broadcast_to(hv[SUBLANES - 1:SUBLANES, :], (SUBLANES, c))
        hs = jnp.concatenate(states, axis=0)
        o_ref[t0:t0 + chunk, :] = (
            hs * y_ref[t0:t0 + chunk, :].astype(F32)).astype(o_ref.dtype)
        return h

    h = jnp.zeros((SUBLANES, c), F32)
    for ci in range(s_len // chunk):
        h = chunk_body(ci, h)


def _rglru(xb, yb, conv_w, conv_b, gate_w, gate_b, lru_lambda, *, batch, seq, chunk, name):
    width = xb.shape[-1]
    c = width // LRU_BLOCKS
    row = pl.BlockSpec((1, c), lambda b, n: (0, n))
    gate = lambda g: pl.BlockSpec((None, None, c, c), lambda b, n: (g, n, 0, 0))
    gate_bias = lambda g: pl.BlockSpec((None, 1, c), lambda b, n: (g, 0, n))
    return pl.pallas_call(
        functools.partial(_rglru_kernel, chunk=chunk),
        grid=(batch, LRU_BLOCKS),
        in_specs=[pl.BlockSpec((None, seq, c), lambda b, n: (b, 0, n)),
                  pl.BlockSpec((None, seq, c), lambda b, n: (b, 0, n)),
                  pl.BlockSpec((CONV_WIDTH, c), lambda b, n: (0, n)),
                  row, gate(0), gate(1), gate_bias(0), gate_bias(1), row],
        out_specs=pl.BlockSpec((None, seq, c), lambda b, n: (b, 0, n)),
        out_shape=jax.ShapeDtypeStruct((batch, seq, width), BF16),
        scratch_shapes=[pltpu.VMEM((seq + SUBLANES, c), F32)],
        compiler_params=_params(("parallel", "parallel"), VMEM_MIB["rglru"]),
        name=name,
    )(xb.reshape(batch, seq, width), yb.reshape(batch, seq, width), conv_w,
      conv_b.reshape(1, width), gate_w, gate_w, gate_b.reshape(2, 1, width),
      gate_b.reshape(2, 1, width), lru_lambda.reshape(1, width))


def _identity(v):
    return v


def _even_mixer(xf, g_pre, w_in, lq1, lk1, lq2, lk2, sub_g, sinks, w_out, g_post,
                *, batch, seq, lambda_init, riders):
    m = batch * seq
    n_qk = 2 * DIFF_HEADS * HEAD_DIM
    n_v = DIFF_HEADS * 2 * HEAD_DIM
    n_sq = SWA_Q_HEADS * HEAD_DIM
    c0 = 2 * n_qk + n_v + n_sq
    ks = [w_in[:, c0 + h * HEAD_DIM:c0 + (h + 1) * HEAD_DIM] for h in range(SWA_KV_HEADS)]
    c1 = c0 + SWA_KV_HEADS * HEAD_DIM
    vs = [w_in[:, c1 + h * HEAD_DIM:c1 + (h + 1) * HEAD_DIM] for h in range(SWA_KV_HEADS)]
    w_qk = w_in[:, :2 * n_qk].astype(BF16)
    w_v = w_in[:, 2 * n_qk:2 * n_qk + n_v].astype(BF16)
    w_sq = w_in[:, 2 * n_qk + n_v:c0].astype(BF16)
    w_sk = jnp.concatenate([t for k in ks for t in (k, k)], axis=1).astype(BF16)
    w_sv = jnp.concatenate([t for v in vs for t in (v, v)], axis=1).astype(BF16)

    tq = min(DIFF_TILE, seq)
    tn = COL_CHUNK
    qk_scales = tuple(QK_SCALE_LOG2 if c < n_qk else 1.0 for c in range(0, 2 * n_qk, tn))
    sq_scales = (QK_SCALE_LOG2,) * (n_sq // tn)
    qk, vt, sq, sk, svt, *cast = _norm_proj(
        xf, g_pre, [w_qk, w_v, w_sq, w_sk, w_sv], [_identity] * 5, [BF16] * 5,
        tm=min(ROW_TILE, m), tn=tn, col_scales=(qk_scales, None, sq_scales, None, None),
        transposed=(None, tq, None, None, SWA_BLOCK), riders=riders, name="even_in_proj")
    a_out = _diff_attn(qk.reshape(batch, seq, -1), vt, lq1, lk1, lq2, lk2, sub_g, batch=batch,
                       seq=seq, tq=tq, lambda_init=lambda_init, name="diff_attn")
    b_out = _swa(sq.reshape(batch, seq, -1), sk.reshape(batch, seq, -1), svt, sinks,
                 batch=batch, seq=seq, name="swa")
    w_o = w_out.astype(BF16)
    out = _out_proj([a_out.reshape(m, n_v), b_out.reshape(m, n_sq)],
                    [w_o[:n_v], w_o[n_v:]], xf, g_post, tm=min(ROW_TILE, m),
                    sub=SUB_ROWS, name="even_out_proj")
    return out, cast


def _odd_mixer(xf, g_pre, w_in, conv_w, conv_b, gate_w, gate_b, lru_lambda, w_out, g_post,
               *, batch, seq, riders):
    m = batch * seq
    width = w_in.shape[1] // 2
    w_b = w_in.astype(BF16)
    yb, xb, *cast = _norm_proj(xf, g_pre, [w_b[:, :width], w_b[:, width:]],
                               [jax.nn.gelu, _identity], [BF16, F32],
                               tm=min(ROW_TILE, m), tn=COL_CHUNK, riders=riders,
                               name="odd_in_proj")
    hy = _rglru(xb, yb, conv_w, conv_b, (0.5 * gate_w).astype(BF16), 0.5 * gate_b, lru_lambda,
                batch=batch, seq=seq, chunk=min(LRU_CHUNK, seq), name="rglru")
    out = _out_proj([hy.reshape(m, width)], [w_out.astype(BF16)], xf, g_post,
                    tm=min(ROW_TILE, m), sub=SUB_ROWS, name="odd_out_proj")
    return out, cast


def kernel(x, even_w_in, even_lam_q1, even_lam_k1, even_lam_q2, even_lam_k2, even_subln_g,
           even_sinks, even_w_out, odd_w_in, odd_conv_w, odd_conv_b, odd_gate_w, odd_gate_b,
           odd_lru_lambda, odd_w_out, pre_mix_g, post_mix_g, pre_mlp_g, post_mlp_g, mlp_w1,
           mlp_w2):
    batch, seq, d = x.shape
    m = batch * seq
    xf = x.reshape(m, d)
    depth = pre_mix_g.shape[0]
    for layer in range(depth):
        riders = ((mlp_w1, layer), (mlp_w2, layer))
        if layer % 2 == 0:
            e = layer // 2
            lambda_init = 0.8 - 0.6 * math.exp(-0.3 * layer)
            xf, (w1_b, w2_b) = _even_mixer(
                xf, pre_mix_g[layer], even_w_in[e], even_lam_q1[e], even_lam_k1[e],
                even_lam_q2[e], even_lam_k2[e], even_subln_g[e], even_sinks[e], even_w_out[e],
                post_mix_g[layer], batch=batch, seq=seq, lambda_init=lambda_init, riders=riders)
        else:
            o = layer // 2
            xf, (w1_b, w2_b) = _odd_mixer(
                xf, pre_mix_g[layer], odd_w_in[o], odd_conv_w[o], odd_conv_b[o], odd_gate_w[o],
                odd_gate_b[o], odd_lru_lambda[o], odd_w_out[o], post_mix_g[layer],
                batch=batch, seq=seq, riders=riders)
        xf = _mlp(xf, pre_mlp_g[layer], w1_b, w2_b, post_mlp_g[layer],
                  tm=min(ROW_TILE, m), tf=FF_TILE, sub=min(SUB_ROWS, m), name=f"mlp_{layer}")
    return xf.reshape(batch, seq, d)
```
